```python
import math
import jax, jax.numpy as jnp
from jax import lax
import numpy as np

D_MODEL = 1024
BATCH = 4
SEQ = 4096
DEPTH = 4

N_MIXERS = 3
BLOCK = 128
ROPE_THETA = 10000.0
NORM_EPS = 1e-6

A_HEADS = 8
A_HEAD_DIM = D_MODEL // (2 * A_HEADS)
A_V_DIM = 2 * A_HEAD_DIM
A_IN = 4 * A_HEADS * A_HEAD_DIM + A_HEADS * A_V_DIM

B_HEADS = 16
B_HEAD_DIM = D_MODEL // B_HEADS
B_IN = 3 * B_HEADS * B_HEAD_DIM

C_HEADS = 16
C_KV_HEADS = 2
C_GROUP = C_HEADS // C_KV_HEADS
C_HEAD_DIM = 64
WINDOW = 128
C_IN = (C_HEADS + 2 * C_KV_HEADS) * C_HEAD_DIM

ROPE_DIM = 64

FFN_HIDDEN = -(-8 * D_MODEL // (3 * 256)) * 256

N_A = (DEPTH + 2) // 3
N_B = (DEPTH + 1) // 3
N_C = DEPTH // 3

kernel_name = "hybrid_diff_stickbreak_swasink_trunk"


def rms_norm(x, g):
    xf = x.astype(jnp.float32)
    y = xf * lax.rsqrt(jnp.mean(xf * xf, axis=-1, keepdims=True) + NORM_EPS)
    return (y * g.astype(jnp.float32)).astype(x.dtype)


def rope_tables(positions):
    inv = ROPE_THETA ** (-jnp.arange(0, ROPE_DIM, 2, dtype=jnp.float32) / ROPE_DIM)
    ang = positions.astype(jnp.float32)[..., None] * inv
    return jnp.cos(ang), jnp.sin(ang)


def apply_rope(x, cos, sin):
    xf = x.astype(jnp.float32)
    x1, x2 = jnp.split(xf, 2, axis=-1)
    c = cos[:, :, None, :]
    s = sin[:, :, None, :]
    return jnp.concatenate([x1 * c - x2 * s, x2 * c + x1 * s], axis=-1).astype(x.dtype)


def diff_attention(h, w_in, w_out, lam_params, subln_g, cos, sin, layer_idx):
    b, s, _ = h.shape
    nblk = s // BLOCK
    qkv = h @ w_in
    nq = 2 * A_HEADS * A_HEAD_DIM
    q, k, v = jnp.split(qkv, [nq, 2 * nq], axis=-1)
    q = apply_rope(q.reshape(b, s, 2 * A_HEADS, A_HEAD_DIM), cos, sin).reshape(b, s, A_HEADS, 2, A_HEAD_DIM)
    k = apply_rope(k.reshape(b, s, 2 * A_HEADS, A_HEAD_DIM), cos, sin).reshape(b, s, A_HEADS, 2, A_HEAD_DIM)
    v = v.reshape(b, s, A_HEADS, A_V_DIM)
    lam_init = 0.8 - 0.6 * math.exp(-0.3 * layer_idx)
    lp = lam_params.astype(jnp.float32)
    lam = jnp.exp(jnp.sum(lp[0] * lp[1])) - jnp.exp(jnp.sum(lp[2] * lp[3])) + lam_init
    scale = A_HEAD_DIM ** -0.5
    kpos = jnp.arange(s)
    q_blocks = jnp.moveaxis(q.reshape(b, nblk, BLOCK, A_HEADS, 2, A_HEAD_DIM), 1, 0)

    def block(args):
        qb, i = args
        sc = jnp.einsum('bqhcd,bkhcd->bhcqk', qb, k).astype(jnp.float32) * scale
        qpos = i * BLOCK + jnp.arange(BLOCK)
        causal = kpos[None, :] <= qpos[:, None]
        p = jax.nn.softmax(jnp.where(causal, sc, -jnp.inf), axis=-1)
        wts = p[:, :, 0] - lam * p[:, :, 1]
        return jnp.einsum('bhqk,bkhe->bqhe', wts.astype(v.dtype), v)

    o = lax.map(block, (q_blocks, jnp.arange(nblk)))
    o = jnp.moveaxis(o, 0, 1).reshape(b, s, A_HEADS, A_V_DIM)
    o = rms_norm(o, subln_g) * (1.0 - lam_init)
    return o.reshape(b, s, A_HEADS * A_V_DIM) @ w_out


def stick_breaking_attention(h, w_in, w_out):
    b, s, _ = h.shape
    nblk = s // BLOCK
    qkv = h @ w_in
    q, k, v = jnp.split(qkv, 3, axis=-1)
    q = q.reshape(b, s, B_HEADS, B_HEAD_DIM)
    k = k.reshape(b, s, B_HEADS, B_HEAD_DIM)
    v = v.reshape(b, s, B_HEADS, B_HEAD_DIM)
    scale = B_HEAD_DIM ** -0.5
    kpos = jnp.arange(s)
    q_blocks = jnp.moveaxis(q.reshape(b, nblk, BLOCK, B_HEADS, B_HEAD_DIM), 1, 0)

    def block(args):
        qb, i = args
        z = jnp.einsum('bqhd,bkhd->bhqk', qb, k).astype(jnp.float32) * scale
        qpos = i * BLOCK + jnp.arange(BLOCK)
        strict = kpos[None, :] < qpos[:, None]
        log_beta = jax.nn.log_sigmoid(z)
        log_1m_beta = jnp.where(strict, jax.nn.log_sigmoid(-z), 0.0)
        suffix = lax.cumsum(log_1m_beta, axis=3, reverse=True) - log_1m_beta
        a = jnp.where(strict, jnp.exp(log_beta + suffix), 0.0)
        return jnp.einsum('bhqk,bkhd->bqhd', a.astype(v.dtype), v)

    o = lax.map(block, (q_blocks, jnp.arange(nblk)))
    o = jnp.moveaxis(o, 0, 1).reshape(b, s, B_HEADS * B_HEAD_DIM)
    return o @ w_out


def sliding_window_sink_attention(h, w_in, w_out, sinks, cos, sin):
    b, s, _ = h.shape
    nblk = s // WINDOW
    qkv = h @ w_in
    nq = C_HEADS * C_HEAD_DIM
    nkv = C_KV_HEADS * C_HEAD_DIM
    q, k, v = jnp.split(qkv, [nq, nq + nkv], axis=-1)
    q = apply_rope(q.reshape(b, s, C_HEADS, C_HEAD_DIM), cos, sin)
    k = apply_rope(k.reshape(b, s, C_KV_HEADS, C_HEAD_DIM), cos, sin)
    v = v.reshape(b, s, C_KV_HEADS, C_HEAD_DIM)
    qb = q.reshape(b, nblk, WINDOW, C_KV_HEADS, C_GROUP, C_HEAD_DIM)
    kb = k.reshape(b, nblk, WINDOW, C_KV_HEADS, C_HEAD_DIM)
    vb = v.reshape(b, nblk, WINDOW, C_KV_HEADS, C_HEAD_DIM)
    pad = ((0, 0), (1, 0), (0, 0), (0, 0), (0, 0))
    kwin = jnp.concatenate([jnp.pad(kb[:, :-1], pad), kb], axis=2)
    vwin = jnp.concatenate([jnp.pad(vb[:, :-1], pad), vb], axis=2)
    sc = jnp.einsum('bnqcgd,bnkcd->bncgqk', qb, kwin).astype(jnp.float32) * (C_HEAD_DIM ** -0.5)
    qi = jnp.arange(WINDOW)[:, None]
    ki = jnp.arange(2 * WINDOW)[None, :]
    rel = WINDOW + qi - ki
    band = (rel >= 0) & (rel < WINDOW)
    in_seq = (jnp.arange(nblk)[:, None] * WINDOW + jnp.arange(2 * WINDOW)[None, :] - WINDOW) >= 0
    mask = band[None] & in_seq[:, None, :]
    sc = jnp.where(mask[None, :, None, None], sc, -jnp.inf)
    sink = sinks.astype(jnp.float32).reshape(C_KV_HEADS, C_GROUP)[None, None, :, :, None, None]
    m = jnp.maximum(jnp.max(sc, axis=-1, keepdims=True), sink)
    p = jnp.exp(sc - m)
    probs = p / (jnp.sum(p, axis=-1, keepdims=True) + jnp.exp(sink - m))
    o = jnp.einsum('bncgqk,bnkcd->bnqcgd', probs.astype(v.dtype), vwin)
    return o.reshape(b, s, C_HEADS * C_HEAD_DIM) @ w_out


def swiglu(h, w_gate, w_up, w_down):
    return (jax.nn.silu(h @ w_gate) * (h @ w_up)) @ w_down


def setup_inputs(seed: int = 0) -> dict:
    key = jax.random.key(seed)
    ks = jax.random.split(key, 16)
    f32 = jnp.float32

    def dense(k, shape, fan_in):
        return jax.random.normal(k, shape, f32) * (fan_in ** -0.5)

    x = jax.random.normal(ks[0], (BATCH, SEQ, D_MODEL), f32)
    positions = jnp.broadcast_to(jnp.arange(SEQ, dtype=jnp.int32)[None, :], (BATCH, SEQ))
    norm_gains = 1.0 + 0.02 * jax.random.normal(ks[1], (DEPTH, 4, D_MODEL), f32)
    a_w_in = dense(ks[2], (N_A, D_MODEL, A_IN), D_MODEL)
    a_w_out = dense(ks[3], (N_A, A_HEADS * A_V_DIM, D_MODEL), A_HEADS * A_V_DIM)
    a_lambda = 0.1 * jax.random.normal(ks[4], (N_A, 4, A_HEAD_DIM), f32)
    a_subln = 1.0 + 0.02 * jax.random.normal(ks[5], (N_A, A_V_DIM), f32)
    b_w_in = dense(ks[6], (N_B, D_MODEL, B_IN), D_MODEL)
    b_w_out = dense(ks[7], (N_B, B_HEADS * B_HEAD_DIM, D_MODEL), B_HEADS * B_HEAD_DIM)
    c_w_in = dense(ks[8], (N_C, D_MODEL, C_IN), D_MODEL)
    c_w_out = dense(ks[9], (N_C, C_HEADS * C_HEAD_DIM, D_MODEL), C_HEADS * C_HEAD_DIM)
    c_sinks = 0.5 * jax.random.normal(ks[10], (N_C, C_HEADS), f32)
    ffn_w_gate = dense(ks[11], (DEPTH, D_MODEL, FFN_HIDDEN), D_MODEL)
    ffn_w_up = dense(ks[12], (DEPTH, D_MODEL, FFN_HIDDEN), D_MODEL)
    ffn_w_down = dense(ks[13], (DEPTH, FFN_HIDDEN, D_MODEL), FFN_HIDDEN)
    return {"x": x, "positions": positions, "norm_gains": norm_gains,
            "a_w_in": a_w_in, "a_w_out": a_w_out, "a_lambda": a_lambda, "a_subln": a_subln,
            "b_w_in": b_w_in, "b_w_out": b_w_out,
            "c_w_in": c_w_in, "c_w_out": c_w_out, "c_sinks": c_sinks,
            "ffn_w_gate": ffn_w_gate, "ffn_w_up": ffn_w_up, "ffn_w_down": ffn_w_down}


def reference(x, positions, norm_gains, a_w_in, a_w_out, a_lambda, a_subln,
              b_w_in, b_w_out, c_w_in, c_w_out, c_sinks,
              ffn_w_gate, ffn_w_up, ffn_w_down):
    cos, sin = rope_tables(positions)
    for i in range(DEPTH):
        kind = i % N_MIXERS
        inst = i // N_MIXERS
        hn = rms_norm(x, norm_gains[i, 0])
        if kind == 0:
            m = diff_attention(hn, a_w_in[inst], a_w_out[inst], a_lambda[inst], a_subln[inst], cos, sin, i)
        elif kind == 1:
            m = stick_breaking_attention(hn, b_w_in[inst], b_w_out[inst])
        else:
            m = sliding_window_sink_attention(hn, c_w_in[inst], c_w_out[inst], c_sinks[inst], cos, sin)
        x = x + rms_norm(m, norm_gains[i, 1])
        f = swiglu(rms_norm(x, norm_gains[i, 2]), ffn_w_gate[i], ffn_w_up[i], ffn_w_down[i])
        x = x + rms_norm(f, norm_gains[i, 3])
    return x
```

```python
import functools
import math

import jax
import jax.numpy as jnp
from jax import lax
from jax.experimental import pallas as pl
from jax.experimental.pallas import tpu as pltpu

F32 = jnp.float32
BF16 = jnp.bfloat16

D_MODEL = 1024
N_MIXERS = 3
ROPE_THETA = 10000.0
ROPE_DIM = 64
NORM_EPS = 1e-6
HEAD_DIM = 64
ATTN_SCALE = HEAD_DIM ** -0.5
A_HEADS = 8
B_HEADS = 16
C_HEADS = 16
C_KV_HEADS = 2
WINDOW = 128

LANES = 128
NEG_BIG = -1e30
VMEM_LIMIT = 56 * 1024 * 1024

ROW_TILE = 512
ATTN_TQ = 256
ATTN_TK = 256
C_TQ = 512


def _params(sem):
    return pltpu.CompilerParams(dimension_semantics=sem, vmem_limit_bytes=VMEM_LIMIT)


def _rms(x, g):
    ms = jnp.mean(x * x, axis=-1, keepdims=True)
    return x * lax.rsqrt(ms + NORM_EPS) * g


def _dot(a, b):
    return jnp.dot(a, b, preferred_element_type=F32)


def _dot_nt(a, b):
    return lax.dot_general(a, b, (((1,), (1,)), ((), ())), preferred_element_type=F32)


def _rope_table_kernel(pos_ref, inv_ref, sign_ref, cos_ref, sin_ref):
    ang = pos_ref[...].astype(F32) * inv_ref[...]
    cos_ref[...] = jnp.cos(ang)
    sin_ref[...] = jnp.sin(ang) * sign_ref[...]


def _rope_tables(positions):
    m = positions.size
    inv = ROPE_THETA ** (-jnp.arange(0, ROPE_DIM, 2, dtype=F32) / ROPE_DIM)
    inv128 = jnp.tile(inv, LANES // (ROPE_DIM // 2)).reshape(1, LANES)
    half = ROPE_DIM // 2
    sign = jnp.where((jnp.arange(LANES) % ROPE_DIM) < half, -1.0, 1.0).astype(F32).reshape(1, LANES)
    tm = 2048
    return pl.pallas_call(
        _rope_table_kernel,
        out_shape=(jax.ShapeDtypeStruct((m, LANES), F32), jax.ShapeDtypeStruct((m, LANES), F32)),
        grid=(m // tm,),
        in_specs=[pl.BlockSpec((tm, 1), lambda i: (i, 0)),
                  pl.BlockSpec((1, LANES), lambda i: (0, 0)),
                  pl.BlockSpec((1, LANES), lambda i: (0, 0))],
        out_specs=(pl.BlockSpec((tm, LANES), lambda i: (i, 0)),
                   pl.BlockSpec((tm, LANES), lambda i: (i, 0))),
        compiler_params=_params(("arbitrary",)),
        name="rope_tables",
    )(positions.reshape(m, 1), inv128, sign)


def _swap_halves(x):
    lane = lax.broadcasted_iota(jnp.int32, x.shape, 1)
    fwd = pltpu.roll(x, ROPE_DIM // 2, 1)
    bwd = pltpu.roll(x, LANES - ROPE_DIM // 2, 1)
    return jnp.where((lane & (ROPE_DIM - 1)) < ROPE_DIM // 2, bwd, fwd)


def _in_proj_kernel(*refs, n_tiles, n_rope, n_q, chunk):
    if n_rope:
        x_ref, g_ref, w_ref, cos_ref, sin_ref, o_ref = refs
    else:
        x_ref, g_ref, w_ref, o_ref = refs
    hn = _rms(x_ref[...], g_ref[...]).astype(BF16)
    if n_rope:
        cos = cos_ref[...]
        sin = sin_ref[...]
    tiles_per_chunk = chunk // LANES
    for c in range(n_tiles // tiles_per_chunk):
        acc = _dot(hn, w_ref[:, c * chunk:(c + 1) * chunk])
        for t in range(tiles_per_chunk):
            tile = c * tiles_per_chunk + t
            a = acc[:, t * LANES:(t + 1) * LANES]
            if tile < n_rope:
                a = a * cos + _swap_halves(a) * sin
            if tile < n_q:
                a = a * ATTN_SCALE
            o_ref[:, tile * LANES:(tile + 1) * LANES] = a.astype(BF16)


def _in_proj(x, g, w, tables, *, n_rope, n_q):
    m = x.shape[0]
    n = w.shape[1]
    tm = ROW_TILE
    chunk = 256
    kern = functools.partial(_in_proj_kernel, n_tiles=n // LANES, n_rope=n_rope, n_q=n_q, chunk=chunk)
    in_specs = [pl.BlockSpec((tm, D_MODEL), lambda i: (i, 0)),
                pl.BlockSpec((1, D_MODEL), lambda i: (0, 0)),
                pl.BlockSpec((D_MODEL, n), lambda i: (0, 0))]
    args = [x, g.reshape(1, D_MODEL), w]
    if n_rope:
        in_specs += [pl.BlockSpec((tm, LANES), lambda i: (i, 0)),
                     pl.BlockSpec((tm, LANES), lambda i: (i, 0))]
        args += list(tables)
    return pl.pallas_call(
        kern,
        out_shape=jax.ShapeDtypeStruct((m, n), BF16),
        grid=(m // tm,),
        in_specs=in_specs,
        out_specs=pl.BlockSpec((tm, n), lambda i: (i, 0)),
        compiler_params=_params(("arbitrary",)),
        name="in_proj",
    )(*args)


def _out_proj_kernel(m_ref, w_ref, g_ref, x_ref, o_ref):
    y = _dot(m_ref[...], w_ref[...])
    o_ref[...] = x_ref[...] + _rms(y, g_ref[...])


def _out_proj(mix, w, g, x):
    m = x.shape[0]
    tm = ROW_TILE
    return pl.pallas_call(
        _out_proj_kernel,
        out_shape=jax.ShapeDtypeStruct((m, D_MODEL), F32),
        grid=(m // tm,),
        in_specs=[pl.BlockSpec((tm, D_MODEL), lambda i: (i, 0)),
                  pl.BlockSpec((D_MODEL, D_MODEL), lambda i: (0, 0)),
                  pl.BlockSpec((1, D_MODEL), lambda i: (0, 0)),
                  pl.BlockSpec((tm, D_MODEL), lambda i: (i, 0))],
        out_specs=pl.BlockSpec((tm, D_MODEL), lambda i: (i, 0)),
        compiler_params=_params(("arbitrary",)),
        name="out_proj",
    )(mix, w, g.reshape(1, D_MODEL), x)


def _ffn_kernel(x_ref, gin_ref, wg_ref, wu_ref, wd_ref, gout_ref, o_ref, act_ref, *, chunk):
    x = x_ref[...]
    hn = _rms(x, gin_ref[...]).astype(BF16)
    hidden = wg_ref.shape[1]
    for c in range(hidden // chunk):
        cols = slice(c * chunk, (c + 1) * chunk)
        gate = _dot(hn, wg_ref[:, cols])
        up = _dot(hn, wu_ref[:, cols])
        act_ref[:, cols] = (gate * jax.nn.sigmoid(gate) * up).astype(BF16)
    f = _dot(act_ref[...], wd_ref[...])
    o_ref[...] = x + _rms(f, gout_ref[...])


def _ffn(x, gin, wg, wu, wd, gout):
    m = x.shape[0]
    hidden = wg.shape[1]
    tm = ROW_TILE
    const = lambda i: (0, 0)
    return pl.pallas_call(
        functools.partial(_ffn_kernel, chunk=256),
        out_shape=jax.ShapeDtypeStruct((m, D_MODEL), F32),
        grid=(m // tm,),
        in_specs=[pl.BlockSpec((tm, D_MODEL), lambda i: (i, 0)),
                  pl.BlockSpec((1, D_MODEL), const),
                  pl.BlockSpec((D_MODEL, hidden), const, pipeline_mode=pl.Buffered(1)),
                  pl.BlockSpec((D_MODEL, hidden), const, pipeline_mode=pl.Buffered(1)),
                  pl.BlockSpec((hidden, D_MODEL), const, pipeline_mode=pl.Buffered(1)),
                  pl.BlockSpec((1, D_MODEL), const)],
        out_specs=pl.BlockSpec((tm, D_MODEL), lambda i: (i, 0)),
        scratch_shapes=[pltpu.VMEM((tm, hidden), BF16)],
        compiler_params=_params(("arbitrary",)),
        name="ffn",
    )(x, gin.reshape(1, D_MODEL), wg, wu, wd, gout.reshape(1, D_MODEL))


def _diff_attn_kernel(lam_ref, subln_ref, q_ref, k_ref, v_ref, o_ref,
                      m_ref, l_ref, acc_ref, *, tq, tk, lam_init):
    i = pl.program_id(2)
    q = q_ref[...]
    lane = lax.broadcasted_iota(jnp.int32, q.shape, 1)
    zero = jnp.zeros_like(q)
    q_half = (jnp.where(lane < HEAD_DIM, q, zero), jnp.where(lane >= HEAD_DIM, q, zero))

    m_ref[...] = jnp.full(m_ref.shape, NEG_BIG, F32)
    l_ref[...] = jnp.zeros(l_ref.shape, F32)
    acc_ref[...] = jnp.zeros(acc_ref.shape, F32)

    def step(j, masked):
        kb = k_ref[pl.ds(pl.multiple_of(j * tk, tk), tk), :]
        vb = v_ref[pl.ds(pl.multiple_of(j * tk, tk), tk), :]
        for c in range(2):
            s = _dot_nt(q_half[c], kb)
            if masked:
                row = lax.broadcasted_iota(jnp.int32, s.shape, 0)
                col = lax.broadcasted_iota(jnp.int32, s.shape, 1)
                s = jnp.where(col <= row, s, NEG_BIG)
            m_old = m_ref[c]
            m_new = jnp.maximum(m_old, jnp.max(s, axis=-1, keepdims=True))
            alpha = jnp.exp(m_old - m_new)
            p = jnp.exp(s - m_new)
            l_ref[c] = alpha * l_ref[c] + jnp.sum(p, axis=-1, keepdims=True)
            acc_ref[c] = alpha * acc_ref[c] + _dot(p.astype(BF16), vb)
            m_ref[c] = m_new

    def body(j, carry):
        step(j, False)
        return carry

    lax.fori_loop(0, i, body, 0)
    step(i, True)

    lp = lam_ref[...]
    lam = (jnp.exp(jnp.sum(lp[0:1] * lp[1:2], axis=-1, keepdims=True))
           - jnp.exp(jnp.sum(lp[2:3] * lp[3:4], axis=-1, keepdims=True)) + lam_init)
    o = acc_ref[0] / l_ref[0] - lam * (acc_ref[1] / l_ref[1])
    o = _rms(o, subln_ref[...]) * (1.0 - lam_init)
    o_ref[...] = o.astype(BF16)


def _diff_attention(qkv, lam_params, subln_g, *, batch, seq, layer_idx):
    tq, tk = ATTN_TQ, ATTN_TK
    assert tq == tk
    nq = seq // tq
    lam_init = 0.8 - 0.6 * math.exp(-0.3 * layer_idx)
    kern = functools.partial(_diff_attn_kernel, tq=tq, tk=tk, lam_init=lam_init)
    return pl.pallas_call(
        kern,
        out_shape=jax.ShapeDtypeStruct((batch * seq, A_HEADS * LANES), BF16),
        grid=(batch, A_HEADS, nq),
        in_specs=[pl.BlockSpec((4, HEAD_DIM), lambda b, h, i: (0, 0)),
                  pl.BlockSpec((1, LANES), lambda b, h, i: (0, 0)),
                  pl.BlockSpec((tq, LANES), lambda b, h, i: (b * nq + i, h)),
                  pl.BlockSpec((seq, LANES), lambda b, h, i: (b, A_HEADS + h)),
                  pl.BlockSpec((seq, LANES), lambda b, h, i: (b, 2 * A_HEADS + h))],
        out_specs=pl.BlockSpec((tq, LANES), lambda b, h, i: (b * nq + i, h)),
        scratch_shapes=[pltpu.VMEM((2, tq, 1), F32), pltpu.VMEM((2, tq, 1), F32),
                        pltpu.VMEM((2, tq, LANES), F32)],
        compiler_params=_params(("arbitrary", "arbitrary", "arbitrary")),
        name="diff_attention",
    )(lam_params, subln_g.reshape(1, LANES), qkv, qkv, qkv)


def _stick_kernel(q_ref, k_ref, v_ref, o_ref, carry_ref, acc_ref, *, tq, tk):
    i = pl.program_id(2)
    q = q_ref[...]
    lane = lax.broadcasted_iota(jnp.int32, q.shape, 1)
    zero = jnp.zeros_like(q)
    q_half = (jnp.where(lane < HEAD_DIM, q, zero), jnp.where(lane >= HEAD_DIM, q, zero))

    r = lax.broadcasted_iota(jnp.int32, (tk, tk), 0)
    cidx = lax.broadcasted_iota(jnp.int32, (tk, tk), 1)
    tri = jnp.where(r > cidx, 1.0, 0.0).astype(BF16)

    carry_ref[...] = jnp.zeros(carry_ref.shape, F32)
    acc_ref[...] = jnp.zeros(acc_ref.shape, F32)

    def step(j, masked):
        kb = k_ref[pl.ds(pl.multiple_of(j * tk, tk), tk), :]
        vb = v_ref[pl.ds(pl.multiple_of(j * tk, tk), tk), :]
        for c in range(2):
            z = _dot_nt(q_half[c], kb)
            softplus_neg = jnp.log(1.0 + jnp.exp(-jnp.abs(z)))
            log_beta = jnp.minimum(z, 0.0) - softplus_neg
            log_1m = log_beta - z
            if masked:
                row = lax.broadcasted_iota(jnp.int32, z.shape, 0)
                col = lax.broadcasted_iota(jnp.int32, z.shape, 1)
                strict = col < row
                log_1m = jnp.where(strict, log_1m, 0.0)
            hi = log_1m.astype(BF16)
            lo = (log_1m - hi.astype(F32)).astype(BF16)
            suffix = _dot(hi, tri) + _dot(lo, tri) + carry_ref[c]
            a = jnp.exp(log_beta + suffix)
            if masked:
                a = jnp.where(strict, a, 0.0)
            acc_ref[c] += _dot(a.astype(BF16), vb)
            carry_ref[c] += jnp.sum(log_1m, axis=-1, keepdims=True)

    step(i, True)

    def body(jj, carry):
        step(i - 1 - jj, False)
        return carry

    lax.fori_loop(0, i, body, 0)
    o_ref[...] = jnp.where(lane < HEAD_DIM, acc_ref[0], acc_ref[1]).astype(BF16)


def _stick_attention(qkv, *, batch, seq):
    tq, tk = ATTN_TQ, ATTN_TK
    assert tq == tk
    nq = seq // tq
    n_tiles = B_HEADS * HEAD_DIM // LANES
    kern = functools.partial(_stick_kernel, tq=tq, tk=tk)
    return pl.pallas_call(
        kern,
        out_shape=jax.ShapeDtypeStruct((batch * seq, B_HEADS * HEAD_DIM), BF16),
        grid=(batch, n_tiles, nq),
        in_specs=[pl.BlockSpec((tq, LANES), lambda b, g, i: (b * nq + i, g)),
                  pl.BlockSpec((seq, LANES), lambda b, g, i: (b, n_tiles + g)),
                  pl.BlockSpec((seq, LANES), lambda b, g, i: (b, 2 * n_tiles + g))],
        out_specs=pl.BlockSpec((tq, LANES), lambda b, g, i: (b * nq + i, g)),
        scratch_shapes=[pltpu.VMEM((2, tq, 1), F32), pltpu.VMEM((2, tq, LANES), F32)],
        compiler_params=_params(("arbitrary", "arbitrary", "arbitrary")),
        name="stick_breaking_attention",
    )(qkv, qkv, qkv)


def _swa_kernel(sink_ref, q_ref, kprev_ref, kcur_ref, vprev_ref, vcur_ref, o_ref, *, tq):
    i = pl.program_id(1)
    w = WINDOW
    qi = lax.broadcasted_iota(jnp.int32, (w, 2 * w), 0)
    ki = lax.broadcasted_iota(jnp.int32, (w, 2 * w), 1)
    rel = w + qi - ki
    band = (rel >= 0) & (rel < w)
    lane = lax.broadcasted_iota(jnp.int32, (w, LANES), 1)
    low = lane < HEAD_DIM
    n_qtiles = C_HEADS * HEAD_DIM // LANES
    tiles_per_group = n_qtiles // C_KV_HEADS
    for sub in range(tq // w):
        rows = slice(sub * w, (sub + 1) * w)
        if sub == 0:
            k_prev, v_prev = kprev_ref[...], vprev_ref[...]
            mask = band & ((ki >= w) | (i > 0))
        else:
            prev = slice((sub - 1) * w, sub * w)
            k_prev, v_prev = kcur_ref[prev, :], vcur_ref[prev, :]
            mask = band
        kwin = jnp.concatenate([k_prev, kcur_ref[rows, :]], axis=0)
        vwin = jnp.concatenate([v_prev, vcur_ref[rows, :]], axis=0)
        for t in range(n_qtiles):
            grp = t // tiles_per_group
            kg = kwin[:, grp * LANES:(grp + 1) * LANES]
            vg = vwin[:, grp * LANES:(grp + 1) * LANES]
            q = q_ref[rows, t * LANES:(t + 1) * LANES]
            zero = jnp.zeros_like(q)
            outs = []
            for c in range(2):
                qc = jnp.where(low, q, zero) if c == 0 else jnp.where(low, zero, q)
                sc = jnp.where(mask, _dot_nt(qc, kg), NEG_BIG)
                sink = sink_ref[2 * t + c]
                mx = jnp.maximum(jnp.max(sc, axis=-1, keepdims=True), sink)
                p = jnp.exp(sc - mx)
                denom = jnp.sum(p, axis=-1, keepdims=True) + jnp.exp(sink - mx)
                outs.append(_dot(p.astype(BF16), vg) / denom)
            o_ref[rows, t * LANES:(t + 1) * LANES] = jnp.where(low, outs[0], outs[1]).astype(BF16)


def _swa_attention(qkv, sinks, *, batch, seq):
    tq = C_TQ
    w = WINDOW
    nq = seq // tq
    per = tq // w
    nq_cols = C_HEADS * HEAD_DIM
    kcol = nq_cols // (2 * LANES)
    vcol = kcol + 1

    def prev_map(col):
        return lambda b, i: (jnp.maximum((b * nq + i) * per - 1, 0), col)

    def cur_map(col):
        return lambda b, i: (b * nq + i, col)

    return pl.pallas_call(
        functools.partial(_swa_kernel, tq=tq),
        out_shape=jax.ShapeDtypeStruct((batch * seq, nq_cols), BF16),
        grid=(batch, nq),
        in_specs=[pl.BlockSpec(memory_space=pltpu.SMEM),
                  pl.BlockSpec((tq, nq_cols), lambda b, i: (b * nq + i, 0)),
                  pl.BlockSpec((w, 2 * LANES), prev_map(kcol)),
                  pl.BlockSpec((tq, 2 * LANES), cur_map(kcol)),
                  pl.BlockSpec((w, 2 * LANES), prev_map(vcol)),
                  pl.BlockSpec((tq, 2 * LANES), cur_map(vcol))],
        out_specs=pl.BlockSpec((tq, nq_cols), lambda b, i: (b * nq + i, 0)),
        compiler_params=_params(("arbitrary", "arbitrary")),
        name="sliding_window_attention",
    )(sinks, qkv, qkv, qkv, qkv, qkv)


def kernel(x, positions, norm_gains, a_w_in, a_w_out, a_lambda, a_subln, b_w_in, b_w_out,
           c_w_in, c_w_out, c_sinks, ffn_w_gate, ffn_w_up, ffn_w_down):
    batch, seq, d = x.shape
    depth = norm_gains.shape[0]
    tables = _rope_tables(positions)
    h = x.reshape(batch * seq, d)
    for i in range(depth):
        kind = i % N_MIXERS
        inst = i // N_MIXERS
        g = norm_gains[i]
        if kind == 0:
            qkv = _in_proj(h, g[0], a_w_in[inst].astype(BF16), tables, n_rope=16, n_q=8)
            mix = _diff_attention(qkv, a_lambda[inst], a_subln[inst], batch=batch, seq=seq, layer_idx=i)
            w_out = a_w_out[inst]
        elif kind == 1:
            qkv = _in_proj(h, g[0], b_w_in[inst].astype(BF16), None, n_rope=0, n_q=8)
            mix = _stick_attention(qkv, batch=batch, seq=seq)
            w_out = b_w_out[inst]
        else:
            w = c_w_in[inst]
            nq_cols = C_HEADS * HEAD_DIM
            kv = w[:, nq_cols:].reshape(d, 2 * C_KV_HEADS, 1, HEAD_DIM)
            kv = jnp.broadcast_to(kv, (d, 2 * C_KV_HEADS, 2, HEAD_DIM)).reshape(d, 4 * C_KV_HEADS * HEAD_DIM)
            w = jnp.concatenate([w[:, :nq_cols], kv], axis=1).astype(BF16)
            qkv = _in_proj(h, g[0], w, tables, n_rope=10, n_q=8)
            mix = _swa_attention(qkv, c_sinks[inst], batch=batch, seq=seq)
            w_out = c_w_out[inst]
        h = _out_proj(mix, w_out.astype(BF16), g[1], h)
        h = _ffn(h, g[2], ffn_w_gate[i].astype(BF16), ffn_w_up[i].astype(BF16),
                 ffn_w_down[i].astype(BF16), g[3])
    return h.reshape(batch, seq, d)
```

```python
import functools
import math

import jax
import jax.numpy as jnp
from jax import lax
from jax.experimental import pallas as pl
from jax.experimental.pallas import tpu as pltpu

F32 = jnp.float32
BF16 = jnp.bfloat16

D_MODEL = 1024
N_MIXERS = 3
ROPE_THETA = 10000.0
ROPE_DIM = 64
NORM_EPS = 1e-6
HEAD_DIM = 64
ATTN_SCALE = HEAD_DIM ** -0.5
A_HEADS = 8
B_HEADS = 16
C_HEADS = 16
C_KV_HEADS = 2
WINDOW = 128

LANES = 128
NEG_BIG = -1e30
VMEM_LIMIT = 56 * 1024 * 1024

ROW_TILE = 512
A_TQ = 512
A_TK = 256
ATTN_TQ = 256
ATTN_TK = 256
C_TQ = 512
LOG2E = math.log2(math.e)


def _params(sem):
    return pltpu.CompilerParams(dimension_semantics=sem, vmem_limit_bytes=VMEM_LIMIT)


def _rms(x, g):
    ms = jnp.mean(x * x, axis=-1, keepdims=True)
    return x * lax.rsqrt(ms + NORM_EPS) * g


def _dot(a, b):
    return jnp.dot(a, b, preferred_element_type=F32)


def _dot_nt(a, b):
    return lax.dot_general(a, b, (((1,), (1,)), ((), ())), preferred_element_type=F32)


def _rope_table_kernel(pos_ref, inv_ref, sign_ref, cos_ref, sin_ref):
    ang = pos_ref[...].astype(F32) * inv_ref[...]
    cos_ref[...] = jnp.cos(ang)
    sin_ref[...] = jnp.sin(ang) * sign_ref[...]


def _rope_tables(positions):
    m = positions.size
    inv = ROPE_THETA ** (-jnp.arange(0, ROPE_DIM, 2, dtype=F32) / ROPE_DIM)
    inv128 = jnp.tile(inv, LANES // (ROPE_DIM // 2)).reshape(1, LANES)
    half = ROPE_DIM // 2
    sign = jnp.where((jnp.arange(LANES) % ROPE_DIM) < half, -1.0, 1.0).astype(F32).reshape(1, LANES)
    tm = 2048
    return pl.pallas_call(
        _rope_table_kernel,
        out_shape=(jax.ShapeDtypeStruct((m, LANES), F32), jax.ShapeDtypeStruct((m, LANES), F32)),
        grid=(m // tm,),
        in_specs=[pl.BlockSpec((tm, 1), lambda i: (i, 0)),
                  pl.BlockSpec((1, LANES), lambda i: (0, 0)),
                  pl.BlockSpec((1, LANES), lambda i: (0, 0))],
        out_specs=(pl.BlockSpec((tm, LANES), lambda i: (i, 0)),
                   pl.BlockSpec((tm, LANES), lambda i: (i, 0))),
        compiler_params=_params(("arbitrary",)),
        name="rope_tables",
    )(positions.reshape(m, 1), inv128, sign)


def _swap_halves(x):
    lane = lax.broadcasted_iota(jnp.int32, x.shape, 1)
    fwd = pltpu.roll(x, ROPE_DIM // 2, 1)
    bwd = pltpu.roll(x, LANES - ROPE_DIM // 2, 1)
    return jnp.where((lane & (ROPE_DIM - 1)) < ROPE_DIM // 2, bwd, fwd)


def _in_proj_kernel(*refs, n_tiles, n_rope, n_q, q_scale, chunk):
    if n_rope:
        x_ref, g_ref, w_ref, cos_ref, sin_ref, o_ref = refs
    else:
        x_ref, g_ref, w_ref, o_ref = refs
    hn = _rms(x_ref[...], g_ref[...]).astype(BF16)
    if n_rope:
        cos = cos_ref[...]
        sin = sin_ref[...]
    tiles_per_chunk = chunk // LANES
    for c in range(n_tiles // tiles_per_chunk):
        acc = _dot(hn, w_ref[:, c * chunk:(c + 1) * chunk])
        for t in range(tiles_per_chunk):
            tile = c * tiles_per_chunk + t
            a = acc[:, t * LANES:(t + 1) * LANES]
            if tile < n_rope:
                a = a * cos + _swap_halves(a) * sin
            if tile < n_q:
                a = a * q_scale
            o_ref[:, tile * LANES:(tile + 1) * LANES] = a.astype(BF16)


def _in_proj(x, g, w, tables, *, n_rope, n_q, q_scale):
    m = x.shape[0]
    n = w.shape[1]
    tm = ROW_TILE
    chunk = 256
    kern = functools.partial(_in_proj_kernel, n_tiles=n // LANES, n_rope=n_rope, n_q=n_q,
                             q_scale=q_scale, chunk=chunk)
    in_specs = [pl.BlockSpec((tm, D_MODEL), lambda i: (i, 0)),
                pl.BlockSpec((1, D_MODEL), lambda i: (0, 0)),
                pl.BlockSpec((D_MODEL, n), lambda i: (0, 0))]
    args = [x, g.reshape(1, D_MODEL), w]
    if n_rope:
        in_specs += [pl.BlockSpec((tm, LANES), lambda i: (i, 0)),
                     pl.BlockSpec((tm, LANES), lambda i: (i, 0))]
        args += list(tables)
    return pl.pallas_call(
        kern,
        out_shape=jax.ShapeDtypeStruct((m, n), BF16),
        grid=(m // tm,),
        in_specs=in_specs,
        out_specs=pl.BlockSpec((tm, n), lambda i: (i, 0)),
        compiler_params=_params(("arbitrary",)),
        name="in_proj",
    )(*args)


def _out_proj_kernel(m_ref, w_ref, g_ref, x_ref, o_ref):
    y = _dot(m_ref[...], w_ref[...])
    o_ref[...] = x_ref[...] + _rms(y, g_ref[...])


def _out_proj(mix, w, g, x):
    m = x.shape[0]
    tm = ROW_TILE
    return pl.pallas_call(
        _out_proj_kernel,
        out_shape=jax.ShapeDtypeStruct((m, D_MODEL), F32),
        grid=(m // tm,),
        in_specs=[pl.BlockSpec((tm, D_MODEL), lambda i: (i, 0)),
                  pl.BlockSpec((D_MODEL, D_MODEL), lambda i: (0, 0)),
                  pl.BlockSpec((1, D_MODEL), lambda i: (0, 0)),
                  pl.BlockSpec((tm, D_MODEL), lambda i: (i, 0))],
        out_specs=pl.BlockSpec((tm, D_MODEL), lambda i: (i, 0)),
        compiler_params=_params(("arbitrary",)),
        name="out_proj",
    )(mix, w, g.reshape(1, D_MODEL), x)


def _ffn_kernel(x_ref, gin_ref, wg_ref, wu_ref, wd_ref, gout_ref, o_ref, act_ref, *, chunk):
    x = x_ref[...]
    hn = _rms(x, gin_ref[...]).astype(BF16)
    hidden = wg_ref.shape[1]
    for c in range(hidden // chunk):
        cols = slice(c * chunk, (c + 1) * chunk)
        gate = _dot(hn, wg_ref[:, cols])
        up = _dot(hn, wu_ref[:, cols])
        act_ref[:, cols] = (gate * jax.nn.sigmoid(gate) * up).astype(BF16)
    f = _dot(act_ref[...], wd_ref[...])
    o_ref[...] = x + _rms(f, gout_ref[...])


def _ffn(x, gin, wg, wu, wd, gout):
    m = x.shape[0]
    hidden = wg.shape[1]
    tm = ROW_TILE
    const = lambda i: (0, 0)
    return pl.pallas_call(
        functools.partial(_ffn_kernel, chunk=256),
        out_shape=jax.ShapeDtypeStruct((m, D_MODEL), F32),
        grid=(m // tm,),
        in_specs=[pl.BlockSpec((tm, D_MODEL), lambda i: (i, 0)),
                  pl.BlockSpec((1, D_MODEL), const),
                  pl.BlockSpec((D_MODEL, hidden), const, pipeline_mode=pl.Buffered(1)),
                  pl.BlockSpec((D_MODEL, hidden), const, pipeline_mode=pl.Buffered(1)),
                  pl.BlockSpec((hidden, D_MODEL), const, pipeline_mode=pl.Buffered(1)),
                  pl.BlockSpec((1, D_MODEL), const)],
        out_specs=pl.BlockSpec((tm, D_MODEL), lambda i: (i, 0)),
        scratch_shapes=[pltpu.VMEM((tm, hidden), BF16)],
        compiler_params=_params(("arbitrary",)),
        name="ffn",
    )(x, gin.reshape(1, D_MODEL), wg, wu, wd, gout.reshape(1, D_MODEL))


def _diff_attn_kernel(lam_ref, subln_ref, q_ref, k_ref, v_ref, o_ref,
                      vt_ref, s_ref, m_ref, l_ref, acc_ref, *, tq, tk, lam_init):
    i = pl.program_id(2)

    @pl.when(i == 0)
    def _():
        for c in range(vt_ref.shape[0]):
            vt_ref[c] = v_ref[c * tk:(c + 1) * tk, :].astype(F32).T.astype(BF16)

    q = q_ref[...]
    lane = lax.broadcasted_iota(jnp.int32, q.shape, 1)
    zero = jnp.zeros_like(q)
    qcat = jnp.concatenate([jnp.where(lane < HEAD_DIM, q, zero),
                            jnp.where(lane >= HEAD_DIM, q, zero)], axis=0)

    m_ref[...] = jnp.full(m_ref.shape, NEG_BIG, F32)
    l_ref[...] = jnp.zeros(l_ref.shape, F32)
    acc_ref[...] = jnp.zeros(acc_ref.shape, F32)

    def scores(j):
        kb = k_ref[pl.ds(pl.multiple_of(j * tk, tk), tk), :]
        return _dot_nt(kb, qcat)

    def update(s, j):
        m_old = m_ref[...]
        m_new = jnp.maximum(m_old, jnp.max(s, axis=0, keepdims=True))
        alpha = jnp.exp2(m_old - m_new)
        p = jnp.exp2(s - m_new)
        l_ref[...] = alpha * l_ref[...] + jnp.sum(p, axis=0, keepdims=True)
        acc_ref[...] = alpha * acc_ref[...] + _dot(vt_ref[j], p.astype(BF16))
        m_ref[...] = m_new

    s_ref[0] = scores(0)

    def body(t, carry):
        j = 2 * t
        s_ref[1] = scores(j + 1)
        update(s_ref[0], j)
        s_ref[0] = scores(j + 2)
        update(s_ref[1], j + 1)
        return carry

    lax.fori_loop(0, i, body, 0)
    j = 2 * i
    s_ref[1] = scores(j + 1)
    key = lax.broadcasted_iota(jnp.int32, (tk, 2 * tq), 0)
    qry = lax.broadcasted_iota(jnp.int32, (tk, 2 * tq), 1) & (tq - 1)
    update(jnp.where(key <= qry, s_ref[0], NEG_BIG), j)
    update(jnp.where(key + tk <= qry, s_ref[1], NEG_BIG), j + 1)

    lp = lam_ref[...]
    lam = (jnp.exp(jnp.sum(lp[0:1] * lp[1:2], axis=-1, keepdims=True))
           - jnp.exp(jnp.sum(lp[2:3] * lp[3:4], axis=-1, keepdims=True)) + lam_init)
    o_both = acc_ref[...] / l_ref[...]
    o_t = o_both[:, :tq] - lam * o_both[:, tq:]
    o = _rms(o_t.T, subln_ref[...]) * (1.0 - lam_init)
    o_ref[...] = o.astype(BF16)


def _diff_attention(qkv, lam_params, subln_g, *, batch, seq, layer_idx):
    tq, tk = A_TQ, A_TK
    assert tq == 2 * tk
    nq = seq // tq
    lam_init = 0.8 - 0.6 * math.exp(-0.3 * layer_idx)
    kern = functools.partial(_diff_attn_kernel, tq=tq, tk=tk, lam_init=lam_init)
    return pl.pallas_call(
        kern,
        out_shape=jax.ShapeDtypeStruct((batch * seq, A_HEADS * LANES), BF16),
        grid=(batch, A_HEADS, nq),
        in_specs=[pl.BlockSpec((4, HEAD_DIM), lambda b, h, i: (0, 0)),
                  pl.BlockSpec((1, LANES), lambda b, h, i: (0, 0)),
                  pl.BlockSpec((tq, LANES), lambda b, h, i: (b * nq + i, h)),
                  pl.BlockSpec((seq, LANES), lambda b, h, i: (b, A_HEADS + h)),
                  pl.BlockSpec((seq, LANES), lambda b, h, i: (b, 2 * A_HEADS + h))],
        out_specs=pl.BlockSpec((tq, LANES), lambda b, h, i: (b * nq + i, h)),
        scratch_shapes=[pltpu.VMEM((seq // tk, LANES, tk), BF16),
                        pltpu.VMEM((2, tk, 2 * tq), F32),
                        pltpu.VMEM((1, 2 * tq), F32), pltpu.VMEM((1, 2 * tq), F32),
                        pltpu.VMEM((LANES, 2 * tq), F32)],
        compiler_params=_params(("arbitrary", "arbitrary", "arbitrary")),
        name="diff_attention",
    )(lam_params, subln_g.reshape(1, LANES), qkv, qkv, qkv)


def _stick_kernel(q_ref, k_ref, v_ref, o_ref, carry_ref, acc_ref, *, tq, tk):
    i = pl.program_id(2)
    q = q_ref[...]
    lane = lax.broadcasted_iota(jnp.int32, q.shape, 1)
    zero = jnp.zeros_like(q)
    q_half = (jnp.where(lane < HEAD_DIM, q, zero), jnp.where(lane >= HEAD_DIM, q, zero))

    r = lax.broadcasted_iota(jnp.int32, (tk, tk), 0)
    cidx = lax.broadcasted_iota(jnp.int32, (tk, tk), 1)
    tri = jnp.where(r > cidx, 1.0, 0.0).astype(BF16)

    carry_ref[...] = jnp.zeros(carry_ref.shape, F32)
    acc_ref[...] = jnp.zeros(acc_ref.shape, F32)

    def step(j, masked):
        kb = k_ref[pl.ds(pl.multiple_of(j * tk, tk), tk), :]
        vb = v_ref[pl.ds(pl.multiple_of(j * tk, tk), tk), :]
        for c in range(2):
            z = _dot_nt(q_half[c], kb)
            softplus_neg = jnp.log(1.0 + jnp.exp(-jnp.abs(z)))
            log_beta = jnp.minimum(z, 0.0) - softplus_neg
            log_1m = log_beta - z
            if masked:
                row = lax.broadcasted_iota(jnp.int32, z.shape, 0)
                col = lax.broadcasted_iota(jnp.int32, z.shape, 1)
                strict = col < row
                log_1m = jnp.where(strict, log_1m, 0.0)
            hi = log_1m.astype(BF16)
            lo = (log_1m - hi.astype(F32)).astype(BF16)
            suffix = _dot(hi, tri) + _dot(lo, tri) + carry_ref[c]
            a = jnp.exp(log_beta + suffix)
            if masked:
                a = jnp.where(strict, a, 0.0)
            acc_ref[c] += _dot(a.astype(BF16), vb)
            carry_ref[c] += jnp.sum(log_1m, axis=-1, keepdims=True)

    step(i, True)

    def body(jj, carry):
        step(i - 1 - jj, False)
        return carry

    lax.fori_loop(0, i, body, 0)
    o_ref[...] = jnp.where(lane < HEAD_DIM, acc_ref[0], acc_ref[1]).astype(BF16)


def _stick_attention(qkv, *, batch, seq):
    tq, tk = ATTN_TQ, ATTN_TK
    assert tq == tk
    nq = seq // tq
    n_tiles = B_HEADS * HEAD_DIM // LANES
    kern = functools.partial(_stick_kernel, tq=tq, tk=tk)
    return pl.pallas_call(
        kern,
        out_shape=jax.ShapeDtypeStruct((batch * seq, B_HEADS * HEAD_DIM), BF16),
        grid=(batch, n_tiles, nq),
        in_specs=[pl.BlockSpec((tq, LANES), lambda b, g, i: (b * nq + i, g)),
                  pl.BlockSpec((seq, LANES), lambda b, g, i: (b, n_tiles + g)),
                  pl.BlockSpec((seq, LANES), lambda b, g, i: (b, 2 * n_tiles + g))],
        out_specs=pl.BlockSpec((tq, LANES), lambda b, g, i: (b * nq + i, g)),
        scratch_shapes=[pltpu.VMEM((2, tq, 1), F32), pltpu.VMEM((2, tq, LANES), F32)],
        compiler_params=_params(("arbitrary", "arbitrary", "arbitrary")),
        name="stick_breaking_attention",
    )(qkv, qkv, qkv)


def _swa_kernel(sink_ref, q_ref, kprev_ref, kcur_ref, vprev_ref, vcur_ref, o_ref, *, tq):
    i = pl.program_id(1)
    w = WINDOW
    qi = lax.broadcasted_iota(jnp.int32, (w, 2 * w), 0)
    ki = lax.broadcasted_iota(jnp.int32, (w, 2 * w), 1)
    rel = w + qi - ki
    band = (rel >= 0) & (rel < w)
    lane = lax.broadcasted_iota(jnp.int32, (w, LANES), 1)
    low = lane < HEAD_DIM
    n_qtiles = C_HEADS * HEAD_DIM // LANES
    tiles_per_group = n_qtiles // C_KV_HEADS
    for sub in range(tq // w):
        rows = slice(sub * w, (sub + 1) * w)
        if sub == 0:
            k_prev, v_prev = kprev_ref[...], vprev_ref[...]
            mask = band & ((ki >= w) | (i > 0))
        else:
            prev = slice((sub - 1) * w, sub * w)
            k_prev, v_prev = kcur_ref[prev, :], vcur_ref[prev, :]
            mask = band
        kwin = jnp.concatenate([k_prev, kcur_ref[rows, :]], axis=0)
        vwin = jnp.concatenate([v_prev, vcur_ref[rows, :]], axis=0)
        for t in range(n_qtiles):
            grp = t // tiles_per_group
            kg = kwin[:, grp * LANES:(grp + 1) * LANES]
            vg = vwin[:, grp * LANES:(grp + 1) * LANES]
            q = q_ref[rows, t * LANES:(t + 1) * LANES]
            zero = jnp.zeros_like(q)
            outs = []
            for c in range(2):
                qc = jnp.where(low, q, zero) if c == 0 else jnp.where(low, zero, q)
                sc = jnp.where(mask, _dot_nt(qc, kg), NEG_BIG)
                sink = sink_ref[2 * t + c] * LOG2E
                mx = jnp.maximum(jnp.max(sc, axis=-1, keepdims=True), sink)
                p = jnp.exp2(sc - mx)
                denom = jnp.sum(p, axis=-1, keepdims=True) + jnp.exp2(sink - mx)
                outs.append(_dot(p.astype(BF16), vg) / denom)
            o_ref[rows, t * LANES:(t + 1) * LANES] = jnp.where(low, outs[0], outs[1]).astype(BF16)


def _swa_attention(qkv, sinks, *, batch, seq):
    tq = C_TQ
    w = WINDOW
    nq = seq // tq
    per = tq // w
    nq_cols = C_HEADS * HEAD_DIM
    kcol = nq_cols // (2 * LANES)
    vcol = kcol + 1

    def prev_map(col):
        return lambda b, i: (jnp.maximum((b * nq + i) * per - 1, 0), col)

    def cur_map(col):
        return lambda b, i: (b * nq + i, col)

    return pl.pallas_call(
        functools.partial(_swa_kernel, tq=tq),
        out_shape=jax.ShapeDtypeStruct((batch * seq, nq_cols), BF16),
        grid=(batch, nq),
        in_specs=[pl.BlockSpec(memory_space=pltpu.SMEM),
                  pl.BlockSpec((tq, nq_cols), lambda b, i: (b * nq + i, 0)),
                  pl.BlockSpec((w, 2 * LANES), prev_map(kcol)),
                  pl.BlockSpec((tq, 2 * LANES), cur_map(kcol)),
                  pl.BlockSpec((w, 2 * LANES), prev_map(vcol)),
                  pl.BlockSpec((tq, 2 * LANES), cur_map(vcol))],
        out_specs=pl.BlockSpec((tq, nq_cols), lambda b, i: (b * nq + i, 0)),
        compiler_params=_params(("arbitrary", "arbitrary")),
        name="sliding_window_attention",
    )(sinks, qkv, qkv, qkv, qkv, qkv)


def kernel(x, positions, norm_gains, a_w_in, a_w_out, a_lambda, a_subln, b_w_in, b_w_out,
           c_w_in, c_w_out, c_sinks, ffn_w_gate, ffn_w_up, ffn_w_down):
    batch, seq, d = x.shape
    depth = norm_gains.shape[0]
    tables = _rope_tables(positions)
    h = x.reshape(batch * seq, d)
    for i in range(depth):
        kind = i % N_MIXERS
        inst = i // N_MIXERS
        g = norm_gains[i]
        if kind == 0:
            qkv = _in_proj(h, g[0], a_w_in[inst].astype(BF16), tables, n_rope=16, n_q=8,
                           q_scale=ATTN_SCALE * LOG2E)
            mix = _diff_attention(qkv, a_lambda[inst], a_subln[inst], batch=batch, seq=seq, layer_idx=i)
            w_out = a_w_out[inst]
        elif kind == 1:
            qkv = _in_proj(h, g[0], b_w_in[inst].astype(BF16), None, n_rope=0, n_q=8,
                           q_scale=ATTN_SCALE)
            mix = _stick_attention(qkv, batch=batch, seq=seq)
            w_out = b_w_out[inst]
        else:
            w = c_w_in[inst]
            nq_cols = C_HEADS * HEAD_DIM
            kv = w[:, nq_cols:].reshape(d, 2 * C_KV_HEADS, 1, HEAD_DIM)
            kv = jnp.broadcast_to(kv, (d, 2 * C_KV_HEADS, 2, HEAD_DIM)).reshape(d, 4 * C_KV_HEADS * HEAD_DIM)
            w = jnp.concatenate([w[:, :nq_cols], kv], axis=1).astype(BF16)
            qkv = _in_proj(h, g[0], w, tables, n_rope=10, n_q=8, q_scale=ATTN_SCALE * LOG2E)
            mix = _swa_attention(qkv, c_sinks[inst], batch=batch, seq=seq)
            w_out = c_w_out[inst]
        h = _out_proj(mix, w_out.astype(BF16), g[1], h)
        h = _ffn(h, g[2], ffn_w_gate[i].astype(BF16), ffn_w_up[i].astype(BF16),
                 ffn_w_down[i].astype(BF16), g[3])
    return h.reshape(batch, seq, d)
```

```python
import functools
import math

import jax
import jax.numpy as jnp
from jax import lax
from jax.experimental import pallas as pl
from jax.experimental.pallas import tpu as pltpu

F32 = jnp.float32
BF16 = jnp.bfloat16

D_MODEL = 1024
N_MIXERS = 3
ROPE_THETA = 10000.0
ROPE_DIM = 64
NORM_EPS = 1e-6
HEAD_DIM = 64
ATTN_SCALE = HEAD_DIM ** -0.5
A_HEADS = 8
B_HEADS = 16
C_HEADS = 16
C_KV_HEADS = 2
WINDOW = 128

LANES = 128
NEG_BIG = -1e30
VMEM_LIMIT = 56 * 1024 * 1024

ROW_TILE = 512
A_TQ = 512
A_TK = 256
B_TQ = 512
B_TK = 256
C_TQ = 512
LOG2E = math.log2(math.e)


def _params(sem):
    return pltpu.CompilerParams(dimension_semantics=sem, vmem_limit_bytes=VMEM_LIMIT)


def _rms(x, g):
    ms = jnp.mean(x * x, axis=-1, keepdims=True)
    return x * lax.rsqrt(ms + NORM_EPS) * g


def _dot(a, b):
    return jnp.dot(a, b, preferred_element_type=F32)


def _dot_nt(a, b):
    return lax.dot_general(a, b, (((1,), (1,)), ((), ())), preferred_element_type=F32)


def _rope_table_kernel(pos_ref, inv_ref, sign_ref, cos_ref, sin_ref):
    ang = pos_ref[...].astype(F32) * inv_ref[...]
    cos_ref[...] = jnp.cos(ang)
    sin_ref[...] = jnp.sin(ang) * sign_ref[...]


def _rope_tables(positions):
    m = positions.size
    inv = ROPE_THETA ** (-jnp.arange(0, ROPE_DIM, 2, dtype=F32) / ROPE_DIM)
    inv128 = jnp.tile(inv, LANES // (ROPE_DIM // 2)).reshape(1, LANES)
    half = ROPE_DIM // 2
    sign = jnp.where((jnp.arange(LANES) % ROPE_DIM) < half, -1.0, 1.0).astype(F32).reshape(1, LANES)
    tm = 2048
    return pl.pallas_call(
        _rope_table_kernel,
        out_shape=(jax.ShapeDtypeStruct((m, LANES), F32), jax.ShapeDtypeStruct((m, LANES), F32)),
        grid=(m // tm,),
        in_specs=[pl.BlockSpec((tm, 1), lambda i: (i, 0)),
                  pl.BlockSpec((1, LANES), lambda i: (0, 0)),
                  pl.BlockSpec((1, LANES), lambda i: (0, 0))],
        out_specs=(pl.BlockSpec((tm, LANES), lambda i: (i, 0)),
                   pl.BlockSpec((tm, LANES), lambda i: (i, 0))),
        compiler_params=_params(("arbitrary",)),
        name="rope_tables",
    )(positions.reshape(m, 1), inv128, sign)


def _swap_halves(x):
    lane = lax.broadcasted_iota(jnp.int32, x.shape, 1)
    fwd = pltpu.roll(x, ROPE_DIM // 2, 1)
    bwd = pltpu.roll(x, LANES - ROPE_DIM // 2, 1)
    return jnp.where((lane & (ROPE_DIM - 1)) < ROPE_DIM // 2, bwd, fwd)


def _in_proj_kernel(*refs, n_tiles, n_rope, n_q, q_scale, chunk):
    if n_rope:
        x_ref, g_ref, w_ref, cos_ref, sin_ref, o_ref = refs
    else:
        x_ref, g_ref, w_ref, o_ref = refs
    hn = _rms(x_ref[...], g_ref[...]).astype(BF16)
    if n_rope:
        cos = cos_ref[...]
        sin = sin_ref[...]
    tiles_per_chunk = chunk // LANES
    for c in range(n_tiles // tiles_per_chunk):
        acc = _dot(hn, w_ref[:, c * chunk:(c + 1) * chunk])
        for t in range(tiles_per_chunk):
            tile = c * tiles_per_chunk + t
            a = acc[:, t * LANES:(t + 1) * LANES]
            if tile < n_rope:
                a = a * cos + _swap_halves(a) * sin
            if tile < n_q:
                a = a * q_scale
            o_ref[:, tile * LANES:(tile + 1) * LANES] = a.astype(BF16)


def _in_proj(x, g, w, tables, *, n_rope, n_q, q_scale):
    m = x.shape[0]
    n = w.shape[1]
    tm = ROW_TILE
    chunk = 256
    kern = functools.partial(_in_proj_kernel, n_tiles=n // LANES, n_rope=n_rope, n_q=n_q,
                             q_scale=q_scale, chunk=chunk)
    in_specs = [pl.BlockSpec((tm, D_MODEL), lambda i: (i, 0)),
                pl.BlockSpec((1, D_MODEL), lambda i: (0, 0)),
                pl.BlockSpec((D_MODEL, n), lambda i: (0, 0))]
    args = [x, g.reshape(1, D_MODEL), w]
    if n_rope:
        in_specs += [pl.BlockSpec((tm, LANES), lambda i: (i, 0)),
                     pl.BlockSpec((tm, LANES), lambda i: (i, 0))]
        args += list(tables)
    return pl.pallas_call(
        kern,
        out_shape=jax.ShapeDtypeStruct((m, n), BF16),
        grid=(m // tm,),
        in_specs=in_specs,
        out_specs=pl.BlockSpec((tm, n), lambda i: (i, 0)),
        compiler_params=_params(("arbitrary",)),
        name="in_proj",
    )(*args)


def _out_proj_kernel(m_ref, w_ref, g_ref, x_ref, o_ref):
    y = _dot(m_ref[...], w_ref[...])
    o_ref[...] = x_ref[...] + _rms(y, g_ref[...])


def _out_proj(mix, w, g, x):
    m = x.shape[0]
    tm = ROW_TILE
    return pl.pallas_call(
        _out_proj_kernel,
        out_shape=jax.ShapeDtypeStruct((m, D_MODEL), F32),
        grid=(m // tm,),
        in_specs=[pl.BlockSpec((tm, D_MODEL), lambda i: (i, 0)),
                  pl.BlockSpec((D_MODEL, D_MODEL), lambda i: (0, 0)),
                  pl.BlockSpec((1, D_MODEL), lambda i: (0, 0)),
                  pl.BlockSpec((tm, D_MODEL), lambda i: (i, 0))],
        out_specs=pl.BlockSpec((tm, D_MODEL), lambda i: (i, 0)),
        compiler_params=_params(("arbitrary",)),
        name="out_proj",
    )(mix, w, g.reshape(1, D_MODEL), x)


def _ffn_kernel(x_ref, gin_ref, wg_ref, wu_ref, wd_ref, gout_ref, o_ref, act_ref, *, chunk):
    x = x_ref[...]
    hn = _rms(x, gin_ref[...]).astype(BF16)
    hidden = wg_ref.shape[1]
    for c in range(hidden // chunk):
        cols = slice(c * chunk, (c + 1) * chunk)
        gate = _dot(hn, wg_ref[:, cols])
        up = _dot(hn, wu_ref[:, cols])
        act_ref[:, cols] = (gate * jax.nn.sigmoid(gate) * up).astype(BF16)
    f = _dot(act_ref[...], wd_ref[...])
    o_ref[...] = x + _rms(f, gout_ref[...])


def _ffn(x, gin, wg, wu, wd, gout):
    m = x.shape[0]
    hidden = wg.shape[1]
    tm = ROW_TILE
    const = lambda i: (0, 0)
    return pl.pallas_call(
        functools.partial(_ffn_kernel, chunk=256),
        out_shape=jax.ShapeDtypeStruct((m, D_MODEL), F32),
        grid=(m // tm,),
        in_specs=[pl.BlockSpec((tm, D_MODEL), lambda i: (i, 0)),
                  pl.BlockSpec((1, D_MODEL), const),
                  pl.BlockSpec((D_MODEL, hidden), const, pipeline_mode=pl.Buffered(1)),
                  pl.BlockSpec((D_MODEL, hidden), const, pipeline_mode=pl.Buffered(1)),
                  pl.BlockSpec((hidden, D_MODEL), const, pipeline_mode=pl.Buffered(1)),
                  pl.BlockSpec((1, D_MODEL), const)],
        out_specs=pl.BlockSpec((tm, D_MODEL), lambda i: (i, 0)),
        scratch_shapes=[pltpu.VMEM((tm, hidden), BF16)],
        compiler_params=_params(("arbitrary",)),
        name="ffn",
    )(x, gin.reshape(1, D_MODEL), wg, wu, wd, gout.reshape(1, D_MODEL))


def _diff_attn_kernel(lam_ref, subln_ref, q_ref, k_ref, v_ref, o_ref,
                      vt_ref, s_ref, m_ref, l_ref, acc_ref, *, tq, tk, lam_init):
    i = pl.program_id(2)

    @pl.when(i == 0)
    def _():
        for c in range(vt_ref.shape[0]):
            vt_ref[c] = v_ref[c * tk:(c + 1) * tk, :].astype(F32).T.astype(BF16)

    q = q_ref[...]
    lane = lax.broadcasted_iota(jnp.int32, q.shape, 1)
    zero = jnp.zeros_like(q)
    qcat = jnp.concatenate([jnp.where(lane < HEAD_DIM, q, zero),
                            jnp.where(lane >= HEAD_DIM, q, zero)], axis=0)

    m_ref[...] = jnp.full(m_ref.shape, NEG_BIG, F32)
    l_ref[...] = jnp.zeros(l_ref.shape, F32)
    acc_ref[...] = jnp.zeros(acc_ref.shape, F32)

    def scores(j):
        kb = k_ref[pl.ds(pl.multiple_of(j * tk, tk), tk), :]
        return _dot_nt(kb, qcat)

    def update(s, j):
        m_old = m_ref[...]
        m_new = jnp.maximum(m_old, jnp.max(s, axis=0, keepdims=True))
        alpha = jnp.exp2(m_old - m_new)
        p = jnp.exp2(s - m_new)
        l_ref[...] = alpha * l_ref[...] + jnp.sum(p, axis=0, keepdims=True)
        acc_ref[...] = alpha * acc_ref[...] + _dot(vt_ref[j], p.astype(BF16))
        m_ref[...] = m_new

    s_ref[0] = scores(0)

    def body(t, carry):
        j = 2 * t
        s_ref[1] = scores(j + 1)
        update(s_ref[0], j)
        s_ref[0] = scores(j + 2)
        update(s_ref[1], j + 1)
        return carry

    lax.fori_loop(0, i, body, 0)
    j = 2 * i
    s_ref[1] = scores(j + 1)
    key = lax.broadcasted_iota(jnp.int32, (tk, 2 * tq), 0)
    qry = lax.broadcasted_iota(jnp.int32, (tk, 2 * tq), 1) & (tq - 1)
    update(jnp.where(key <= qry, s_ref[0], NEG_BIG), j)
    update(jnp.where(key + tk <= qry, s_ref[1], NEG_BIG), j + 1)

    lp = lam_ref[...]
    lam = (jnp.exp(jnp.sum(lp[0:1] * lp[1:2], axis=-1, keepdims=True))
           - jnp.exp(jnp.sum(lp[2:3] * lp[3:4], axis=-1, keepdims=True)) + lam_init)
    o_both = acc_ref[...] / l_ref[...]
    o_t = o_both[:, :tq] - lam * o_both[:, tq:]
    o = _rms(o_t.T, subln_ref[...]) * (1.0 - lam_init)
    o_ref[...] = o.astype(BF16)


def _diff_attention(qkv, lam_params, subln_g, *, batch, seq, layer_idx):
    tq, tk = A_TQ, A_TK
    assert tq == 2 * tk
    nq = seq // tq
    lam_init = 0.8 - 0.6 * math.exp(-0.3 * layer_idx)
    kern = functools.partial(_diff_attn_kernel, tq=tq, tk=tk, lam_init=lam_init)
    return pl.pallas_call(
        kern,
        out_shape=jax.ShapeDtypeStruct((batch * seq, A_HEADS * LANES), BF16),
        grid=(batch, A_HEADS, nq),
        in_specs=[pl.BlockSpec((4, HEAD_DIM), lambda b, h, i: (0, 0)),
                  pl.BlockSpec((1, LANES), lambda b, h, i: (0, 0)),
                  pl.BlockSpec((tq, LANES), lambda b, h, i: (b * nq + i, h)),
                  pl.BlockSpec((seq, LANES), lambda b, h, i: (b, A_HEADS + h)),
                  pl.BlockSpec((seq, LANES), lambda b, h, i: (b, 2 * A_HEADS + h))],
        out_specs=pl.BlockSpec((tq, LANES), lambda b, h, i: (b * nq + i, h)),
        scratch_shapes=[pltpu.VMEM((seq // tk, LANES, tk), BF16),
                        pltpu.VMEM((2, tk, 2 * tq), F32),
                        pltpu.VMEM((1, 2 * tq), F32), pltpu.VMEM((1, 2 * tq), F32),
                        pltpu.VMEM((LANES, 2 * tq), F32)],
        compiler_params=_params(("arbitrary", "arbitrary", "arbitrary")),
        name="diff_attention",
    )(lam_params, subln_g.reshape(1, LANES), qkv, qkv, qkv)


def _stick_kernel(q_ref, k_ref, v_ref, o_ref, kp_ref, vt_ref, s_ref, carry_ref, acc_ref, *, tq, tk):
    i = pl.program_id(2)
    n = 2 * tq
    nv = tk // 8

    @pl.when(i == 0)
    def _():
        r = lax.broadcasted_iota(jnp.int32, (tk, tk), 0)
        c = lax.broadcasted_iota(jnp.int32, (tk, tk), 1)
        perm = jnp.where(c == (r & 7) * nv + (r >> 3), 1.0, 0.0).astype(BF16)

        def prep(blk, carry):
            rows = pl.ds(pl.multiple_of(blk * tk, tk), tk)
            kp_ref[rows, :] = _dot(perm, k_ref[rows, :]).astype(BF16)
            vt_ref[blk] = _dot(perm, v_ref[rows, :]).T.astype(BF16)
            return carry

        lax.fori_loop(0, vt_ref.shape[0], prep, 0)

    q = q_ref[...]
    lane = lax.broadcasted_iota(jnp.int32, q.shape, 1)
    zero = jnp.zeros_like(q)
    qcat = jnp.concatenate([jnp.where(lane < HEAD_DIM, q, zero),
                            jnp.where(lane >= HEAD_DIM, q, zero)], axis=0)

    carry_ref[...] = jnp.ones(carry_ref.shape, F32)
    acc_ref[...] = jnp.zeros(acc_ref.shape, F32)

    def scores(j):
        kb = kp_ref[pl.ds(pl.multiple_of(j * tk, tk), tk), :]
        return _dot_nt(kb, qcat)

    row = lax.broadcasted_iota(jnp.int32, (tk, n), 0)
    kpos = (row & 7) * nv + (row >> 3)
    qry = lax.broadcasted_iota(jnp.int32, (tk, n), 1) & (tq - 1)
    sub = lax.broadcasted_iota(jnp.int32, (8, n), 0)

    def update(zh, j, diag):
        half_tanh = 0.5 * jnp.tanh(zh)
        beta = 0.5 + half_tanh
        omb = 0.5 - half_tanh
        if diag is not None:
            strict = kpos + diag * tk < qry
            beta = jnp.where(strict, beta, 0.0)
            omb = jnp.where(strict, omb, 1.0)
        run = jnp.ones((8, n), F32)
        bw = [None] * nv
        for v in reversed(range(nv)):
            rows = slice(v * 8, (v + 1) * 8)
            bw[v] = beta[rows] * run
            run = run * omb[rows]
        later = jnp.ones((8, n), F32)
        for sp in range(7, 0, -1):
            later = jnp.where(sub < sp, later * run[sp:sp + 1, :], later)
        carry = carry_ref[...]
        scale = later * carry
        a = jnp.concatenate([bw[v] * scale for v in range(nv)], axis=0).astype(BF16)
        acc_ref[...] += _dot(vt_ref[j], a)
        carry_ref[...] = carry * (later[0:1, :] * run[0:1, :])

    top = 2 * i + 1
    s_ref[0] = scores(top)
    s_ref[1] = scores(top - 1)
    update(s_ref[0], top, 1)
    s_ref[0] = scores(jnp.maximum(top - 2, 0))
    update(s_ref[1], top - 1, 0)

    def body(t, c):
        j = top - 2 - 2 * t
        s_ref[1] = scores(j - 1)
        update(s_ref[0], j, None)
        s_ref[0] = scores(jnp.maximum(j - 2, 0))
        update(s_ref[1], j - 1, None)
        return c

    lax.fori_loop(0, i, body, 0)
    acc = acc_ref[...]
    head_row = lax.broadcasted_iota(jnp.int32, (LANES, tq), 0) < HEAD_DIM
    o_t = jnp.where(head_row, acc[:, :tq], acc[:, tq:])
    o_ref[...] = o_t.T.astype(BF16)


def _stick_attention(qkv, *, batch, seq):
    tq, tk = B_TQ, B_TK
    assert tq == 2 * tk
    nq = seq // tq
    n_tiles = B_HEADS * HEAD_DIM // LANES
    kern = functools.partial(_stick_kernel, tq=tq, tk=tk)
    return pl.pallas_call(
        kern,
        out_shape=jax.ShapeDtypeStruct((batch * seq, B_HEADS * HEAD_DIM), BF16),
        grid=(batch, n_tiles, nq),
        in_specs=[pl.BlockSpec((tq, LANES), lambda b, g, i: (b * nq + i, g)),
                  pl.BlockSpec((seq, LANES), lambda b, g, i: (b, n_tiles + g)),
                  pl.BlockSpec((seq, LANES), lambda b, g, i: (b, 2 * n_tiles + g))],
        out_specs=pl.BlockSpec((tq, LANES), lambda b, g, i: (b * nq + i, g)),
        scratch_shapes=[pltpu.VMEM((seq, LANES), BF16),
                        pltpu.VMEM((seq // tk, LANES, tk), BF16),
                        pltpu.VMEM((2, tk, 2 * tq), F32),
                        pltpu.VMEM((1, 2 * tq), F32),
                        pltpu.VMEM((LANES, 2 * tq), F32)],
        compiler_params=_params(("arbitrary", "arbitrary", "arbitrary")),
        name="stick_breaking_attention",
    )(qkv, qkv, qkv)


def _swa_kernel(sink_ref, q_ref, kprev_ref, kcur_ref, vprev_ref, vcur_ref, o_ref, *, tq):
    i = pl.program_id(1)
    w = WINDOW
    qi = lax.broadcasted_iota(jnp.int32, (w, 2 * w), 0)
    ki = lax.broadcasted_iota(jnp.int32, (w, 2 * w), 1)
    rel = w + qi - ki
    band = (rel >= 0) & (rel < w)
    lane = lax.broadcasted_iota(jnp.int32, (w, LANES), 1)
    low = lane < HEAD_DIM
    n_qtiles = C_HEADS * HEAD_DIM // LANES
    tiles_per_group = n_qtiles // C_KV_HEADS
    for sub in range(tq // w):
        rows = slice(sub * w, (sub + 1) * w)
        if sub == 0:
            k_prev, v_prev = kprev_ref[...], vprev_ref[...]
            mask = band & ((ki >= w) | (i > 0))
        else:
            prev = slice((sub - 1) * w, sub * w)
            k_prev, v_prev = kcur_ref[prev, :], vcur_ref[prev, :]
            mask = band
        kwin = jnp.concatenate([k_prev, kcur_ref[rows, :]], axis=0)
        vwin = jnp.concatenate([v_prev, vcur_ref[rows, :]], axis=0)
        for t in range(n_qtiles):
            grp = t // tiles_per_group
            kg = kwin[:, grp * LANES:(grp + 1) * LANES]
            vg = vwin[:, grp * LANES:(grp + 1) * LANES]
            q = q_ref[rows, t * LANES:(t + 1) * LANES]
            zero = jnp.zeros_like(q)
            outs = []
            for c in range(2):
                qc = jnp.where(low, q, zero) if c == 0 else jnp.where(low, zero, q)
                sc = jnp.where(mask, _dot_nt(qc, kg), NEG_BIG)
                sink = sink_ref[2 * t + c] * LOG2E
                mx = jnp.maximum(jnp.max(sc, axis=-1, keepdims=True), sink)
                p = jnp.exp2(sc - mx)
                denom = jnp.sum(p, axis=-1, keepdims=True) + jnp.exp2(sink - mx)
                outs.append(_dot(p.astype(BF16), vg) / denom)
            o_ref[rows, t * LANES:(t + 1) * LANES] = jnp.where(low, outs[0], outs[1]).astype(BF16)


def _swa_attention(qkv, sinks, *, batch, seq):
    tq = C_TQ
    w = WINDOW
    nq = seq // tq
    per = tq // w
    nq_cols = C_HEADS * HEAD_DIM
    kcol = nq_cols // (2 * LANES)
    vcol = kcol + 1

    def prev_map(col):
        return lambda b, i: (jnp.maximum((b * nq + i) * per - 1, 0), col)

    def cur_map(col):
        return lambda b, i: (b * nq + i, col)

    return pl.pallas_call(
        functools.partial(_swa_kernel, tq=tq),
        out_shape=jax.ShapeDtypeStruct((batch * seq, nq_cols), BF16),
        grid=(batch, nq),
        in_specs=[pl.BlockSpec(memory_space=pltpu.SMEM),
                  pl.BlockSpec((tq, nq_cols), lambda b, i: (b * nq + i, 0)),
                  pl.BlockSpec((w, 2 * LANES), prev_map(kcol)),
                  pl.BlockSpec((tq, 2 * LANES), cur_map(kcol)),
                  pl.BlockSpec((w, 2 * LANES), prev_map(vcol)),
                  pl.BlockSpec((tq, 2 * LANES), cur_map(vcol))],
        out_specs=pl.BlockSpec((tq, nq_cols), lambda b, i: (b * nq + i, 0)),
        compiler_params=_params(("arbitrary", "arbitrary")),
        name="sliding_window_attention",
    )(sinks, qkv, qkv, qkv, qkv, qkv)


def kernel(x, positions, norm_gains, a_w_in, a_w_out, a_lambda, a_subln, b_w_in, b_w_out,
           c_w_in, c_w_out, c_sinks, ffn_w_gate, ffn_w_up, ffn_w_down):
    batch, seq, d = x.shape
    depth = norm_gains.shape[0]
    tables = _rope_tables(positions)
    h = x.reshape(batch * seq, d)
    for i in range(depth):
        kind = i % N_MIXERS
        inst = i // N_MIXERS
        g = norm_gains[i]
        if kind == 0:
            qkv = _in_proj(h, g[0], a_w_in[inst].astype(BF16), tables, n_rope=16, n_q=8,
                           q_scale=ATTN_SCALE * LOG2E)
            mix = _diff_attention(qkv, a_lambda[inst], a_subln[inst], batch=batch, seq=seq, layer_idx=i)
            w_out = a_w_out[inst]
        elif kind == 1:
            qkv = _in_proj(h, g[0], b_w_in[inst].astype(BF16), None, n_rope=0, n_q=8,
                           q_scale=ATTN_SCALE * 0.5)
            mix = _stick_attention(qkv, batch=batch, seq=seq)
            w_out = b_w_out[inst]
        else:
            w = c_w_in[inst]
            nq_cols = C_HEADS * HEAD_DIM
            kv = w[:, nq_cols:].reshape(d, 2 * C_KV_HEADS, 1, HEAD_DIM)
            kv = jnp.broadcast_to(kv, (d, 2 * C_KV_HEADS, 2, HEAD_DIM)).reshape(d, 4 * C_KV_HEADS * HEAD_DIM)
            w = jnp.concatenate([w[:, :nq_cols], kv], axis=1).astype(BF16)
            qkv = _in_proj(h, g[0], w, tables, n_rope=10, n_q=8, q_scale=ATTN_SCALE * LOG2E)
            mix = _swa_attention(qkv, c_sinks[inst], batch=batch, seq=seq)
            w_out = c_w_out[inst]
        h = _out_proj(mix, w_out.astype(BF16), g[1], h)
        h = _ffn(h, g[2], ffn_w_gate[i].astype(BF16), ffn_w_up[i].astype(BF16),
                 ffn_w_down[i].astype(BF16), g[3])
    return h.reshape(batch, seq, d)
```

```python
import functools
import math

import jax
import jax.numpy as jnp
from jax import lax
from jax.experimental import pallas as pl
from jax.experimental.pallas import tpu as pltpu

F32 = jnp.float32
BF16 = jnp.bfloat16

D_MODEL = 1024
N_MIXERS = 3
ROPE_THETA = 10000.0
ROPE_DIM = 64
NORM_EPS = 1e-6
HEAD_DIM = 64
ATTN_SCALE = HEAD_DIM ** -0.5
A_HEADS = 8
B_HEADS = 16
C_HEADS = 16
C_KV_HEADS = 2
WINDOW = 128

LANES = 128
NEG_BIG = -1e30
NEGLIGIBLE = 2.0 ** -100
VMEM_LIMIT = 56 * 1024 * 1024

ROW_TILE = 512
A_TQ = 512
A_TK = 256
B_TQ = 512
B_TK = 256
C_TQ = 512
LOG2E = math.log2(math.e)


def _params(sem):
    return pltpu.CompilerParams(dimension_semantics=sem, vmem_limit_bytes=VMEM_LIMIT)


def _rms(x, g):
    ms = jnp.mean(x * x, axis=-1, keepdims=True)
    return x * lax.rsqrt(ms + NORM_EPS) * g


def _dot(a, b):
    return jnp.dot(a, b, preferred_element_type=F32)


def _dot_nt(a, b):
    return lax.dot_general(a, b, (((1,), (1,)), ((), ())), preferred_element_type=F32)


def _rope_table_kernel(pos_ref, inv_ref, sign_ref, cos_ref, sin_ref):
    ang = pos_ref[...].astype(F32) * inv_ref[...]
    cos_ref[...] = jnp.cos(ang)
    sin_ref[...] = jnp.sin(ang) * sign_ref[...]


def _rope_tables(positions):
    m = positions.size
    inv = ROPE_THETA ** (-jnp.arange(0, ROPE_DIM, 2, dtype=F32) / ROPE_DIM)
    inv128 = jnp.tile(inv, LANES // (ROPE_DIM // 2)).reshape(1, LANES)
    half = ROPE_DIM // 2
    sign = jnp.where((jnp.arange(LANES) % ROPE_DIM) < half, -1.0, 1.0).astype(F32).reshape(1, LANES)
    tm = 2048
    return pl.pallas_call(
        _rope_table_kernel,
        out_shape=(jax.ShapeDtypeStruct((m, LANES), F32), jax.ShapeDtypeStruct((m, LANES), F32)),
        grid=(m // tm,),
        in_specs=[pl.BlockSpec((tm, 1), lambda i: (i, 0)),
                  pl.BlockSpec((1, LANES), lambda i: (0, 0)),
                  pl.BlockSpec((1, LANES), lambda i: (0, 0))],
        out_specs=(pl.BlockSpec((tm, LANES), lambda i: (i, 0)),
                   pl.BlockSpec((tm, LANES), lambda i: (i, 0))),
        compiler_params=_params(("arbitrary",)),
        name="rope_tables",
    )(positions.reshape(m, 1), inv128, sign)


def _swap_halves(x):
    lane = lax.broadcasted_iota(jnp.int32, x.shape, 1)
    fwd = pltpu.roll(x, ROPE_DIM // 2, 1)
    bwd = pltpu.roll(x, LANES - ROPE_DIM // 2, 1)
    return jnp.where((lane & (ROPE_DIM - 1)) < ROPE_DIM // 2, bwd, fwd)


def _in_proj_kernel(*refs, n_tiles, n_rope, n_q, q_scale, chunk):
    if n_rope:
        x_ref, g_ref, w_ref, cos_ref, sin_ref, o_ref = refs
    else:
        x_ref, g_ref, w_ref, o_ref = refs
    hn = _rms(x_ref[...], g_ref[...]).astype(BF16)
    if n_rope:
        cos = cos_ref[...]
        sin = sin_ref[...]
    tiles_per_chunk = chunk // LANES
    for c in range(n_tiles // tiles_per_chunk):
        acc = _dot(hn, w_ref[:, c * chunk:(c + 1) * chunk])
        for t in range(tiles_per_chunk):
            tile = c * tiles_per_chunk + t
            a = acc[:, t * LANES:(t + 1) * LANES]
            if tile < n_rope:
                a = a * cos + _swap_halves(a) * sin
            if tile < n_q:
                a = a * q_scale
            o_ref[:, tile * LANES:(tile + 1) * LANES] = a.astype(BF16)


def _in_proj(x, g, w, tables, *, n_rope, n_q, q_scale):
    m = x.shape[0]
    n = w.shape[1]
    tm = ROW_TILE
    chunk = 256
    kern = functools.partial(_in_proj_kernel, n_tiles=n // LANES, n_rope=n_rope, n_q=n_q,
                             q_scale=q_scale, chunk=chunk)
    in_specs = [pl.BlockSpec((tm, D_MODEL), lambda i: (i, 0)),
                pl.BlockSpec((1, D_MODEL), lambda i: (0, 0)),
                pl.BlockSpec((D_MODEL, n), lambda i: (0, 0))]
    args = [x, g.reshape(1, D_MODEL), w]
    if n_rope:
        in_specs += [pl.BlockSpec((tm, LANES), lambda i: (i, 0)),
                     pl.BlockSpec((tm, LANES), lambda i: (i, 0))]
        args += list(tables)
    return pl.pallas_call(
        kern,
        out_shape=jax.ShapeDtypeStruct((m, n), BF16),
        grid=(m // tm,),
        in_specs=in_specs,
        out_specs=pl.BlockSpec((tm, n), lambda i: (i, 0)),
        compiler_params=_params(("arbitrary",)),
        name="in_proj",
    )(*args)


def _post_mixer_kernel(mix_ref, wo_ref, gmix_ref, x_ref, gin_ref, wg_ref, wu_ref, wd_ref, gout_ref,
                       o_ref, act_ref, *, chunk):
    x = x_ref[...] + _rms(_dot(mix_ref[...], wo_ref[...]), gmix_ref[...])
    hn = _rms(x, gin_ref[...]).astype(BF16)
    hidden = wg_ref.shape[1]
    for c in range(hidden // chunk):
        cols = slice(c * chunk, (c + 1) * chunk)
        gate = _dot(hn, wg_ref[:, cols])
        up = _dot(hn, wu_ref[:, cols])
        act_ref[:, cols] = (gate * jax.nn.sigmoid(gate) * up).astype(BF16)
    f = _dot(act_ref[...], wd_ref[...])
    o_ref[...] = x + _rms(f, gout_ref[...])


def _post_mixer(mix, wo, gmix, x, gin, wg, wu, wd, gout):
    m = x.shape[0]
    hidden = wg.shape[1]
    tm = ROW_TILE
    const = lambda i: (0, 0)
    rows = lambda i: (i, 0)
    resident = functools.partial(pl.BlockSpec, index_map=const, pipeline_mode=pl.Buffered(1))
    gain = pl.BlockSpec((1, D_MODEL), const)
    return pl.pallas_call(
        functools.partial(_post_mixer_kernel, chunk=256),
        out_shape=jax.ShapeDtypeStruct((m, D_MODEL), F32),
        grid=(m // tm,),
        in_specs=[pl.BlockSpec((tm, D_MODEL), rows),
                  resident((D_MODEL, D_MODEL)),
                  gain,
                  pl.BlockSpec((tm, D_MODEL), rows),
                  gain,
                  resident((D_MODEL, hidden)),
                  resident((D_MODEL, hidden)),
                  resident((hidden, D_MODEL)),
                  gain],
        out_specs=pl.BlockSpec((tm, D_MODEL), rows),
        scratch_shapes=[pltpu.VMEM((tm, hidden), BF16)],
        compiler_params=_params(("arbitrary",)),
        name="out_proj_ffn",
    )(mix, wo, gmix.reshape(1, D_MODEL), x, gin.reshape(1, D_MODEL), wg, wu, wd,
      gout.reshape(1, D_MODEL))


def _diff_attn_kernel(lam_ref, subln_ref, q_ref, k_ref, v_ref, o_ref,
                      vt_ref, s_ref, m_ref, l_ref, acc_ref, *, tq, tk, lam_init):
    i = pl.program_id(2)

    @pl.when(i == 0)
    def _():
        for c in range(vt_ref.shape[0]):
            vt_ref[c] = v_ref[c * tk:(c + 1) * tk, :].astype(F32).T.astype(BF16)

    q = q_ref[...]
    lane = lax.broadcasted_iota(jnp.int32, q.shape, 1)
    zero = jnp.zeros_like(q)
    qcat = jnp.concatenate([jnp.where(lane < HEAD_DIM, q, zero),
                            jnp.where(lane >= HEAD_DIM, q, zero)], axis=0)

    m_ref[...] = jnp.full(m_ref.shape, NEG_BIG, F32)
    l_ref[...] = jnp.zeros(l_ref.shape, F32)
    acc_ref[...] = jnp.zeros(acc_ref.shape, F32)

    def scores(j):
        kb = k_ref[pl.ds(pl.multiple_of(j * tk, tk), tk), :]
        return _dot_nt(kb, qcat)

    def update(s, j):
        m_old = m_ref[...]
        m_new = jnp.maximum(m_old, jnp.max(s, axis=0, keepdims=True))
        alpha = jnp.exp2(m_old - m_new)
        p = jnp.exp2(s - m_new)
        l_ref[...] = alpha * l_ref[...] + jnp.sum(p, axis=0, keepdims=True)
        acc_ref[...] = alpha * acc_ref[...] + _dot(vt_ref[j], p.astype(BF16))
        m_ref[...] = m_new

    s_ref[0] = scores(0)

    def pair(j):
        s_ref[1] = scores(j + 1)
        update(s_ref[0], j)
        s_ref[0] = scores(j + 2)
        update(s_ref[1], j + 1)

    def body(t, carry):
        pair(4 * t)
        pair(4 * t + 2)
        return carry

    lax.fori_loop(0, i >> 1, body, 0)

    @pl.when((i & 1) == 1)
    def _():
        pair(2 * i - 2)

    j = 2 * i
    s_ref[1] = scores(j + 1)
    key = lax.broadcasted_iota(jnp.int32, (tk, 2 * tq), 0)
    qry = lax.broadcasted_iota(jnp.int32, (tk, 2 * tq), 1) & (tq - 1)
    update(jnp.where(key <= qry, s_ref[0], NEG_BIG), j)
    update(jnp.where(key + tk <= qry, s_ref[1], NEG_BIG), j + 1)

    lp = lam_ref[...]
    lam = (jnp.exp(jnp.sum(lp[0:1] * lp[1:2], axis=-1, keepdims=True))
           - jnp.exp(jnp.sum(lp[2:3] * lp[3:4], axis=-1, keepdims=True)) + lam_init)
    o_both = acc_ref[...] / l_ref[...]
    o_t = o_both[:, :tq] - lam * o_both[:, tq:]
    o = _rms(o_t.T, subln_ref[...]) * (1.0 - lam_init)
    o_ref[...] = o.astype(BF16)


def _diff_attention(qkv, lam_params, subln_g, *, batch, seq, layer_idx):
    tq, tk = A_TQ, A_TK
    assert tq == 2 * tk
    nq = seq // tq
    lam_init = 0.8 - 0.6 * math.exp(-0.3 * layer_idx)
    kern = functools.partial(_diff_attn_kernel, tq=tq, tk=tk, lam_init=lam_init)
    return pl.pallas_call(
        kern,
        out_shape=jax.ShapeDtypeStruct((batch * seq, A_HEADS * LANES), BF16),
        grid=(batch, A_HEADS, nq),
        in_specs=[pl.BlockSpec((4, HEAD_DIM), lambda b, h, i: (0, 0)),
                  pl.BlockSpec((1, LANES), lambda b, h, i: (0, 0)),
                  pl.BlockSpec((tq, LANES), lambda b, h, i: (b * nq + i, h)),
                  pl.BlockSpec((seq, LANES), lambda b, h, i: (b, A_HEADS + h)),
                  pl.BlockSpec((seq, LANES), lambda b, h, i: (b, 2 * A_HEADS + h))],
        out_specs=pl.BlockSpec((tq, LANES), lambda b, h, i: (b * nq + i, h)),
        scratch_shapes=[pltpu.VMEM((seq // tk, LANES, tk), BF16),
                        pltpu.VMEM((2, tk, 2 * tq), F32),
                        pltpu.VMEM((1, 2 * tq), F32), pltpu.VMEM((1, 2 * tq), F32),
                        pltpu.VMEM((LANES, 2 * tq), F32)],
        compiler_params=_params(("arbitrary", "arbitrary", "arbitrary")),
        name="diff_attention",
    )(lam_params, subln_g.reshape(1, LANES), qkv, qkv, qkv)


def _stick_kernel(q_ref, k_ref, v_ref, o_ref, kp_ref, vt_ref, s_ref, carry_ref, acc_ref, *, tq, tk):
    i = pl.program_id(2)
    n = 2 * tq
    nv = tk // 8

    @pl.when(i == 0)
    def _():
        r = lax.broadcasted_iota(jnp.int32, (tk, tk), 0)
        c = lax.broadcasted_iota(jnp.int32, (tk, tk), 1)
        perm = jnp.where(c == (r & 7) * nv + (r >> 3), 1.0, 0.0).astype(BF16)

        def prep(blk, carry):
            rows = pl.ds(pl.multiple_of(blk * tk, tk), tk)
            kp_ref[rows, :] = _dot(perm, k_ref[rows, :]).astype(BF16)
            vt_ref[blk] = _dot(perm, v_ref[rows, :]).T.astype(BF16)
            return carry

        lax.fori_loop(0, vt_ref.shape[0], prep, 0)

    q = q_ref[...]
    lane = lax.broadcasted_iota(jnp.int32, q.shape, 1)
    zero = jnp.zeros_like(q)
    qcat = jnp.concatenate([jnp.where(lane < HEAD_DIM, q, zero),
                            jnp.where(lane >= HEAD_DIM, q, zero)], axis=0)

    carry_ref[...] = jnp.ones(carry_ref.shape, F32)
    acc_ref[...] = jnp.zeros(acc_ref.shape, F32)

    def scores(j):
        kb = kp_ref[pl.ds(pl.multiple_of(j * tk, tk), tk), :]
        return _dot_nt(kb, qcat)

    row = lax.broadcasted_iota(jnp.int32, (tk, n), 0)
    kpos = (row & 7) * nv + (row >> 3)
    qry = lax.broadcasted_iota(jnp.int32, (tk, n), 1) & (tq - 1)
    sub = lax.broadcasted_iota(jnp.int32, (8, n), 0)

    def update(zh, j, diag):
        half_tanh = 0.5 * jnp.tanh(zh)
        beta = 0.5 + half_tanh
        omb = 0.5 - half_tanh
        if diag is not None:
            strict = kpos + diag * tk < qry
            beta = jnp.where(strict, beta, 0.0)
            omb = jnp.where(strict, omb, 1.0)
        run = jnp.ones((8, n), F32)
        bw = [None] * nv
        for v in reversed(range(nv)):
            rows = slice(v * 8, (v + 1) * 8)
            bw[v] = beta[rows] * run
            run = run * omb[rows]
        later = jnp.ones((8, n), F32)
        for sp in range(7, 0, -1):
            later = jnp.where(sub < sp, later * run[sp:sp + 1, :], later)
        carry = carry_ref[...]
        scale = later * carry
        a = jnp.concatenate([bw[v] * scale for v in range(nv)], axis=0).astype(BF16)
        acc_ref[...] += _dot(vt_ref[j], a)
        carry_ref[...] = carry * (later[0:1, :] * run[0:1, :])

    top = 2 * i + 1
    s_ref[0] = scores(top)
    s_ref[1] = scores(top - 1)
    update(s_ref[0], top, 1)
    s_ref[0] = scores(jnp.maximum(top - 2, 0))
    update(s_ref[1], top - 1, 0)

    def live():
        return jnp.max(carry_ref[...]) > NEGLIGIBLE

    def cond(state):
        t, alive = state
        return jnp.logical_and(t < i, alive)

    def body(state):
        t, _ = state
        j = top - 2 - 2 * t
        s_ref[1] = scores(j - 1)
        update(s_ref[0], j, None)
        s_ref[0] = scores(jnp.maximum(j - 2, 0))
        update(s_ref[1], j - 1, None)
        return t + 1, live()

    lax.while_loop(cond, body, (0, live()))
    acc = acc_ref[...]
    head_row = lax.broadcasted_iota(jnp.int32, (LANES, tq), 0) < HEAD_DIM
    o_t = jnp.where(head_row, acc[:, :tq], acc[:, tq:])
    o_ref[...] = o_t.T.astype(BF16)


def _stick_attention(qkv, *, batch, seq):
    tq, tk = B_TQ, B_TK
    assert tq == 2 * tk
    nq = seq // tq
    n_tiles = B_HEADS * HEAD_DIM // LANES
    kern = functools.partial(_stick_kernel, tq=tq, tk=tk)
    return pl.pallas_call(
        kern,
        out_shape=jax.ShapeDtypeStruct((batch * seq, B_HEADS * HEAD_DIM), BF16),
        grid=(batch, n_tiles, nq),
        in_specs=[pl.BlockSpec((tq, LANES), lambda b, g, i: (b * nq + i, g)),
                  pl.BlockSpec((seq, LANES), lambda b, g, i: (b, n_tiles + g)),
                  pl.BlockSpec((seq, LANES), lambda b, g, i: (b, 2 * n_tiles + g))],
        out_specs=pl.BlockSpec((tq, LANES), lambda b, g, i: (b * nq + i, g)),
        scratch_shapes=[pltpu.VMEM((seq, LANES), BF16),
                        pltpu.VMEM((seq // tk, LANES, tk), BF16),
                        pltpu.VMEM((2, tk, 2 * tq), F32),
                        pltpu.VMEM((1, 2 * tq), F32),
                        pltpu.VMEM((LANES, 2 * tq), F32)],
        compiler_params=_params(("arbitrary", "arbitrary", "arbitrary")),
        name="stick_breaking_attention",
    )(qkv, qkv, qkv)


def _swa_kernel(sink_ref, q_ref, kprev_ref, kcur_ref, vprev_ref, vcur_ref, o_ref, *, tq):
    i = pl.program_id(1)
    w = WINDOW
    qi = lax.broadcasted_iota(jnp.int32, (w, 2 * w), 0)
    ki = lax.broadcasted_iota(jnp.int32, (w, 2 * w), 1)
    rel = w + qi - ki
    band = (rel >= 0) & (rel < w)
    lane = lax.broadcasted_iota(jnp.int32, (w, LANES), 1)
    low = lane < HEAD_DIM
    n_qtiles = C_HEADS * HEAD_DIM // LANES
    tiles_per_group = n_qtiles // C_KV_HEADS
    for sub in range(tq // w):
        rows = slice(sub * w, (sub + 1) * w)
        if sub == 0:
            k_prev, v_prev = kprev_ref[...], vprev_ref[...]
            mask = band & ((ki >= w) | (i > 0))
        else:
            prev = slice((sub - 1) * w, sub * w)
            k_prev, v_prev = kcur_ref[prev, :], vcur_ref[prev, :]
            mask = band
        kwin = jnp.concatenate([k_prev, kcur_ref[rows, :]], axis=0)
        vwin = jnp.concatenate([v_prev, vcur_ref[rows, :]], axis=0)
        for t in range(n_qtiles):
            grp = t // tiles_per_group
            kg = kwin[:, grp * LANES:(grp + 1) * LANES]
            vg = vwin[:, grp * LANES:(grp + 1) * LANES]
            q = q_ref[rows, t * LANES:(t + 1) * LANES]
            zero = jnp.zeros_like(q)
            outs = []
            for c in range(2):
                qc = jnp.where(low, q, zero) if c == 0 else jnp.where(low, zero, q)
                sc = jnp.where(mask, _dot_nt(qc, kg), NEG_BIG)
                sink = sink_ref[2 * t + c] * LOG2E
                mx = jnp.maximum(jnp.max(sc, axis=-1, keepdims=True), sink)
                p = jnp.exp2(sc - mx)
                denom = jnp.sum(p, axis=-1, keepdims=True) + jnp.exp2(sink - mx)
                outs.append(_dot(p.astype(BF16), vg) / denom)
            o_ref[rows, t * LANES:(t + 1) * LANES] = jnp.where(low, outs[0], outs[1]).astype(BF16)


def _swa_attention(qkv, sinks, *, batch, seq):
    tq = C_TQ
    w = WINDOW
    nq = seq // tq
    per = tq // w
    nq_cols = C_HEADS * HEAD_DIM
    kcol = nq_cols // (2 * LANES)
    vcol = kcol + 1

    def prev_map(col):
        return lambda b, i: (jnp.maximum((b * nq + i) * per - 1, 0), col)

    def cur_map(col):
        return lambda b, i: (b * nq + i, col)

    return pl.pallas_call(
        functools.partial(_swa_kernel, tq=tq),
        out_shape=jax.ShapeDtypeStruct((batch * seq, nq_cols), BF16),
        grid=(batch, nq),
        in_specs=[pl.BlockSpec(memory_space=pltpu.SMEM),
                  pl.BlockSpec((tq, nq_cols), lambda b, i: (b * nq + i, 0)),
                  pl.BlockSpec((w, 2 * LANES), prev_map(kcol)),
                  pl.BlockSpec((tq, 2 * LANES), cur_map(kcol)),
                  pl.BlockSpec((w, 2 * LANES), prev_map(vcol)),
                  pl.BlockSpec((tq, 2 * LANES), cur_map(vcol))],
        out_specs=pl.BlockSpec((tq, nq_cols), lambda b, i: (b * nq + i, 0)),
        compiler_params=_params(("arbitrary", "arbitrary")),
        name="sliding_window_attention",
    )(sinks, qkv, qkv, qkv, qkv, qkv)


def kernel(x, positions, norm_gains, a_w_in, a_w_out, a_lambda, a_subln, b_w_in, b_w_out,
           c_w_in, c_w_out, c_sinks, ffn_w_gate, ffn_w_up, ffn_w_down):
    batch, seq, d = x.shape
    depth = norm_gains.shape[0]
    tables = _rope_tables(positions)
    h = x.reshape(batch * seq, d)
    for i in range(depth):
        kind = i % N_MIXERS
        inst = i // N_MIXERS
        g = norm_gains[i]
        if kind == 0:
            qkv = _in_proj(h, g[0], a_w_in[inst].astype(BF16), tables, n_rope=16, n_q=8,
                           q_scale=ATTN_SCALE * LOG2E)
            mix = _diff_attention(qkv, a_lambda[inst], a_subln[inst], batch=batch, seq=seq, layer_idx=i)
            w_out = a_w_out[inst]
        elif kind == 1:
            qkv = _in_proj(h, g[0], b_w_in[inst].astype(BF16), None, n_rope=0, n_q=8,
                           q_scale=ATTN_SCALE * 0.5)
            mix = _stick_attention(qkv, batch=batch, seq=seq)
            w_out = b_w_out[inst]
        else:
            w = c_w_in[inst]
            nq_cols = C_HEADS * HEAD_DIM
            kv = w[:, nq_cols:].reshape(d, 2 * C_KV_HEADS, 1, HEAD_DIM)
            kv = jnp.broadcast_to(kv, (d, 2 * C_KV_HEADS, 2, HEAD_DIM)).reshape(d, 4 * C_KV_HEADS * HEAD_DIM)
            w = jnp.concatenate([w[:, :nq_cols], kv], axis=1).astype(BF16)
            qkv = _in_proj(h, g[0], w, tables, n_rope=10, n_q=8, q_scale=ATTN_SCALE * LOG2E)
            mix = _swa_attention(qkv, c_sinks[inst], batch=batch, seq=seq)
            w_out = c_w_out[inst]
        h = _post_mixer(mix, w_out.astype(BF16), g[1], h, g[2], ffn_w_gate[i].astype(BF16),
                        ffn_w_up[i].astype(BF16), ffn_w_down[i].astype(BF16), g[3])
    return h.reshape(batch, seq, d)
```

```python
import functools
import math

import jax
import jax.numpy as jnp
from jax import lax
from jax.experimental import pallas as pl
from jax.experimental.pallas import tpu as pltpu

F32 = jnp.float32
BF16 = jnp.bfloat16

D_MODEL = 1024
N_MIXERS = 3
ROPE_THETA = 10000.0
ROPE_DIM = 64
NORM_EPS = 1e-6
HEAD_DIM = 64
ATTN_SCALE = HEAD_DIM ** -0.5
A_HEADS = 8
B_HEADS = 16
C_HEADS = 16
C_KV_HEADS = 2
WINDOW = 128

LANES = 128
NEG_BIG = -1e30
NEGLIGIBLE = 2.0 ** -100
VMEM_LIMIT = 56 * 1024 * 1024

ROW_TILE = 512
A_TQ = 512
A_TK = 256
A_HEADS_PER_STEP = 2
B_TQ = 512
B_TK = 256
C_TQ = 512
LOG2E = math.log2(math.e)


def _params(sem):
    return pltpu.CompilerParams(dimension_semantics=sem, vmem_limit_bytes=VMEM_LIMIT)


def _rms(x, g):
    ms = jnp.mean(x * x, axis=-1, keepdims=True)
    return x * lax.rsqrt(ms + NORM_EPS) * g


def _dot(a, b):
    return jnp.dot(a, b, preferred_element_type=F32)


def _dot_nt(a, b):
    return lax.dot_general(a, b, (((1,), (1,)), ((), ())), preferred_element_type=F32)


def _rope_table_kernel(pos_ref, inv_ref, sign_ref, cos_ref, sin_ref):
    ang = pos_ref[...].astype(F32) * inv_ref[...]
    cos_ref[...] = jnp.cos(ang)
    sin_ref[...] = jnp.sin(ang) * sign_ref[...]


def _rope_tables(positions):
    m = positions.size
    inv = ROPE_THETA ** (-jnp.arange(0, ROPE_DIM, 2, dtype=F32) / ROPE_DIM)
    inv128 = jnp.tile(inv, LANES // (ROPE_DIM // 2)).reshape(1, LANES)
    half = ROPE_DIM // 2
    sign = jnp.where((jnp.arange(LANES) % ROPE_DIM) < half, -1.0, 1.0).astype(F32).reshape(1, LANES)
    tm = 2048
    return pl.pallas_call(
        _rope_table_kernel,
        out_shape=(jax.ShapeDtypeStruct((m, LANES), F32), jax.ShapeDtypeStruct((m, LANES), F32)),
        grid=(m // tm,),
        in_specs=[pl.BlockSpec((tm, 1), lambda i: (i, 0)),
                  pl.BlockSpec((1, LANES), lambda i: (0, 0)),
                  pl.BlockSpec((1, LANES), lambda i: (0, 0))],
        out_specs=(pl.BlockSpec((tm, LANES), lambda i: (i, 0)),
                   pl.BlockSpec((tm, LANES), lambda i: (i, 0))),
        compiler_params=_params(("arbitrary",)),
        name="rope_tables",
    )(positions.reshape(m, 1), inv128, sign)


def _swap_halves(x):
    lane = lax.broadcasted_iota(jnp.int32, x.shape, 1)
    fwd = pltpu.roll(x, ROPE_DIM // 2, 1)
    bwd = pltpu.roll(x, LANES - ROPE_DIM // 2, 1)
    return jnp.where((lane & (ROPE_DIM - 1)) < ROPE_DIM // 2, bwd, fwd)


def _in_proj_kernel(*refs, n_tiles, n_rope, n_q, q_scale, chunk):
    if n_rope:
        x_ref, g_ref, w_ref, cos_ref, sin_ref, o_ref = refs
    else:
        x_ref, g_ref, w_ref, o_ref = refs
    hn = _rms(x_ref[...], g_ref[...]).astype(BF16)
    if n_rope:
        cos = cos_ref[...]
        sin = sin_ref[...]
    tiles_per_chunk = chunk // LANES
    for c in range(n_tiles // tiles_per_chunk):
        acc = _dot(hn, w_ref[:, c * chunk:(c + 1) * chunk])
        for t in range(tiles_per_chunk):
            tile = c * tiles_per_chunk + t
            a = acc[:, t * LANES:(t + 1) * LANES]
            if tile < n_rope:
                a = a * cos + _swap_halves(a) * sin
            if tile < n_q:
                a = a * q_scale
            o_ref[:, tile * LANES:(tile + 1) * LANES] = a.astype(BF16)


def _in_proj(x, g, w, tables, *, n_rope, n_q, q_scale):
    m = x.shape[0]
    n = w.shape[1]
    tm = ROW_TILE
    chunk = 256
    kern = functools.partial(_in_proj_kernel, n_tiles=n // LANES, n_rope=n_rope, n_q=n_q,
                             q_scale=q_scale, chunk=chunk)
    in_specs = [pl.BlockSpec((tm, D_MODEL), lambda i: (i, 0)),
                pl.BlockSpec((1, D_MODEL), lambda i: (0, 0)),
                pl.BlockSpec((D_MODEL, n), lambda i: (0, 0))]
    args = [x, g.reshape(1, D_MODEL), w]
    if n_rope:
        in_specs += [pl.BlockSpec((tm, LANES), lambda i: (i, 0)),
                     pl.BlockSpec((tm, LANES), lambda i: (i, 0))]
        args += list(tables)
    return pl.pallas_call(
        kern,
        out_shape=jax.ShapeDtypeStruct((m, n), BF16),
        grid=(m // tm,),
        in_specs=in_specs,
        out_specs=pl.BlockSpec((tm, n), lambda i: (i, 0)),
        compiler_params=_params(("arbitrary",)),
        name="in_proj",
    )(*args)


def _post_mixer_kernel(mix_ref, wo_ref, gmix_ref, x_ref, gin_ref, wg_ref, wu_ref, wd_ref, gout_ref,
                       o_ref, act_ref, *, chunk):
    x = x_ref[...] + _rms(_dot(mix_ref[...], wo_ref[...]), gmix_ref[...])
    hn = _rms(x, gin_ref[...]).astype(BF16)
    hidden = wg_ref.shape[1]
    for c in range(hidden // chunk):
        cols = slice(c * chunk, (c + 1) * chunk)
        gate = _dot(hn, wg_ref[:, cols])
        up = _dot(hn, wu_ref[:, cols])
        act_ref[:, cols] = (gate * jax.nn.sigmoid(gate) * up).astype(BF16)
    f = _dot(act_ref[...], wd_ref[...])
    o_ref[...] = x + _rms(f, gout_ref[...])


def _post_mixer(mix, wo, gmix, x, gin, wg, wu, wd, gout):
    m = x.shape[0]
    hidden = wg.shape[1]
    tm = ROW_TILE
    const = lambda i: (0, 0)
    rows = lambda i: (i, 0)
    resident = functools.partial(pl.BlockSpec, index_map=const, pipeline_mode=pl.Buffered(1))
    gain = pl.BlockSpec((1, D_MODEL), const)
    return pl.pallas_call(
        functools.partial(_post_mixer_kernel, chunk=256),
        out_shape=jax.ShapeDtypeStruct((m, D_MODEL), F32),
        grid=(m // tm,),
        in_specs=[pl.BlockSpec((tm, D_MODEL), rows),
                  resident((D_MODEL, D_MODEL)),
                  gain,
                  pl.BlockSpec((tm, D_MODEL), rows),
                  gain,
                  resident((D_MODEL, hidden)),
                  resident((D_MODEL, hidden)),
                  resident((hidden, D_MODEL)),
                  gain],
        out_specs=pl.BlockSpec((tm, D_MODEL), rows),
        scratch_shapes=[pltpu.VMEM((tm, hidden), BF16)],
        compiler_params=_params(("arbitrary",)),
        name="out_proj_ffn",
    )(mix, wo, gmix.reshape(1, D_MODEL), x, gin.reshape(1, D_MODEL), wg, wu, wd,
      gout.reshape(1, D_MODEL))


def _diff_attn_kernel(lam_ref, subln_ref, q_ref, k_ref, v_ref, o_ref,
                      vt_ref, s_ref, m_ref, l_ref, acc_ref, *, tq, tk, lam_init):
    i = pl.program_id(2)
    heads = range(vt_ref.shape[0])

    def head_cols(hh):
        return slice(hh * LANES, (hh + 1) * LANES)

    @pl.when(i == 0)
    def _():
        for hh in heads:
            for c in range(vt_ref.shape[1]):
                v_blk = v_ref[c * tk:(c + 1) * tk, head_cols(hh)]
                vt_ref[hh, c] = v_blk.astype(F32).T.astype(BF16)

    lane = lax.broadcasted_iota(jnp.int32, (tq, LANES), 1)
    qcat = []
    for hh in heads:
        q = q_ref[:, head_cols(hh)]
        zero = jnp.zeros_like(q)
        qcat.append(jnp.concatenate([jnp.where(lane < HEAD_DIM, q, zero),
                                     jnp.where(lane >= HEAD_DIM, q, zero)], axis=0))

    m_ref[...] = jnp.full(m_ref.shape, NEG_BIG, F32)
    l_ref[...] = jnp.zeros(l_ref.shape, F32)
    acc_ref[...] = jnp.zeros(acc_ref.shape, F32)

    def scores(hh, j):
        kb = k_ref[pl.ds(pl.multiple_of(j * tk, tk), tk), head_cols(hh)]
        return _dot_nt(kb, qcat[hh])

    def update(hh, s, j):
        m_old = m_ref[hh]
        m_new = jnp.maximum(m_old, jnp.max(s, axis=0, keepdims=True))
        alpha = jnp.exp2(m_old - m_new)
        p = jnp.exp2(s - m_new)
        l_ref[hh] = alpha * l_ref[hh] + jnp.sum(p, axis=0, keepdims=True)
        acc_ref[hh] = alpha * acc_ref[hh] + _dot(vt_ref[hh, j], p.astype(BF16))
        m_ref[hh] = m_new

    for hh in heads:
        s_ref[hh, 0] = scores(hh, 0)

    def pair(j):
        for hh in heads:
            s_ref[hh, 1] = scores(hh, j + 1)
        for hh in heads:
            update(hh, s_ref[hh, 0], j)
        for hh in heads:
            s_ref[hh, 0] = scores(hh, j + 2)
        for hh in heads:
            update(hh, s_ref[hh, 1], j + 1)

    def body(t, carry):
        pair(4 * t)
        pair(4 * t + 2)
        return carry

    lax.fori_loop(0, i >> 1, body, 0)

    @pl.when((i & 1) == 1)
    def _():
        pair(2 * i - 2)

    j = 2 * i
    for hh in heads:
        s_ref[hh, 1] = scores(hh, j + 1)
    key = lax.broadcasted_iota(jnp.int32, (tk, 2 * tq), 0)
    qry = lax.broadcasted_iota(jnp.int32, (tk, 2 * tq), 1) & (tq - 1)
    for hh in heads:
        update(hh, jnp.where(key <= qry, s_ref[hh, 0], NEG_BIG), j)
    for hh in heads:
        update(hh, jnp.where(key + tk <= qry, s_ref[hh, 1], NEG_BIG), j + 1)

    lp = lam_ref[...]
    lam = (jnp.exp(jnp.sum(lp[0:1] * lp[1:2], axis=-1, keepdims=True))
           - jnp.exp(jnp.sum(lp[2:3] * lp[3:4], axis=-1, keepdims=True)) + lam_init)
    for hh in heads:
        o_both = acc_ref[hh] / l_ref[hh]
        o_t = o_both[:, :tq] - lam * o_both[:, tq:]
        o = _rms(o_t.T, subln_ref[...]) * (1.0 - lam_init)
        o_ref[:, head_cols(hh)] = o.astype(BF16)


def _diff_attention(qkv, lam_params, subln_g, *, batch, seq, layer_idx):
    tq, tk = A_TQ, A_TK
    assert tq == 2 * tk
    nq = seq // tq
    lam_init = 0.8 - 0.6 * math.exp(-0.3 * layer_idx)
    kern = functools.partial(_diff_attn_kernel, tq=tq, tk=tk, lam_init=lam_init)
    nh = A_HEADS_PER_STEP
    groups = A_HEADS // nh
    width = nh * LANES
    return pl.pallas_call(
        kern,
        out_shape=jax.ShapeDtypeStruct((batch * seq, A_HEADS * LANES), BF16),
        grid=(batch, groups, nq),
        in_specs=[pl.BlockSpec((4, HEAD_DIM), lambda b, h, i: (0, 0)),
                  pl.BlockSpec((1, LANES), lambda b, h, i: (0, 0)),
                  pl.BlockSpec((tq, width), lambda b, h, i: (b * nq + i, h)),
                  pl.BlockSpec((seq, width), lambda b, h, i: (b, groups + h)),
                  pl.BlockSpec((seq, width), lambda b, h, i: (b, 2 * groups + h))],
        out_specs=pl.BlockSpec((tq, width), lambda b, h, i: (b * nq + i, h)),
        scratch_shapes=[pltpu.VMEM((nh, seq // tk, LANES, tk), BF16),
                        pltpu.VMEM((nh, 2, tk, 2 * tq), F32),
                        pltpu.VMEM((nh, 1, 2 * tq), F32), pltpu.VMEM((nh, 1, 2 * tq), F32),
                        pltpu.VMEM((nh, LANES, 2 * tq), F32)],
        compiler_params=_params(("arbitrary", "arbitrary", "arbitrary")),
        name="diff_attention",
    )(lam_params, subln_g.reshape(1, LANES), qkv, qkv, qkv)


def _stick_kernel(q_ref, k_ref, v_ref, o_ref, kp_ref, vt_ref, s_ref, carry_ref, acc_ref, *, tq, tk):
    i = pl.program_id(2)
    n = 2 * tq
    nv = tk // 8

    @pl.when(i == 0)
    def _():
        r = lax.broadcasted_iota(jnp.int32, (tk, tk), 0)
        c = lax.broadcasted_iota(jnp.int32, (tk, tk), 1)
        perm = jnp.where(c == (r & 7) * nv + (r >> 3), 1.0, 0.0).astype(BF16)

        for blk in range(vt_ref.shape[0]):
            rows = slice(blk * tk, (blk + 1) * tk)
            kv = _dot(perm, jnp.concatenate([k_ref[rows, :], v_ref[rows, :]], axis=1))
            kp_ref[rows, :] = kv[:, :LANES].astype(BF16)
            vt_ref[blk] = kv[:, LANES:].T.astype(BF16)

    q = q_ref[...]
    lane = lax.broadcasted_iota(jnp.int32, q.shape, 1)
    zero = jnp.zeros_like(q)
    qcat = jnp.concatenate([jnp.where(lane < HEAD_DIM, q, zero),
                            jnp.where(lane >= HEAD_DIM, q, zero)], axis=0)

    carry_ref[...] = jnp.ones(carry_ref.shape, F32)
    acc_ref[...] = jnp.zeros(acc_ref.shape, F32)

    def scores(j):
        kb = kp_ref[pl.ds(pl.multiple_of(j * tk, tk), tk), :]
        return _dot_nt(kb, qcat)

    row = lax.broadcasted_iota(jnp.int32, (tk, n), 0)
    kpos = (row & 7) * nv + (row >> 3)
    qry = lax.broadcasted_iota(jnp.int32, (tk, n), 1) & (tq - 1)
    sub = lax.broadcasted_iota(jnp.int32, (8, n), 0)

    def update(zh, j, diag):
        if diag is not None:
            zh = jnp.where(kpos + diag * tk < qry, zh, NEG_BIG)
        half_tanh = 0.5 * jnp.tanh(zh)
        beta = 0.5 + half_tanh
        omb = 0.5 - half_tanh
        run = jnp.ones((8, n), F32)
        bw = [None] * nv
        for v in reversed(range(nv)):
            rows = slice(v * 8, (v + 1) * 8)
            bw[v] = beta[rows] * run
            run = run * omb[rows]
        later = jnp.ones((8, n), F32)
        for sp in range(7, 0, -1):
            later = jnp.where(sub < sp, later * run[sp:sp + 1, :], later)
        carry = carry_ref[...]
        scale = later * carry
        a = jnp.concatenate([bw[v] * scale for v in range(nv)], axis=0).astype(BF16)
        acc_ref[...] += _dot(vt_ref[j], a)
        carry_ref[...] = carry * (later[0:1, :] * run[0:1, :])

    top = 2 * i + 1
    s_ref[0] = scores(top)
    s_ref[1] = scores(top - 1)
    update(s_ref[0], top, 1)
    s_ref[0] = scores(jnp.maximum(top - 2, 0))
    update(s_ref[1], top - 1, 0)

    def live():
        return jnp.max(carry_ref[...]) > NEGLIGIBLE

    def cond(state):
        t, alive = state
        return jnp.logical_and(t < i, alive)

    def body(state):
        t, _ = state
        j = top - 2 - 2 * t
        s_ref[1] = scores(j - 1)
        update(s_ref[0], j, None)
        s_ref[0] = scores(jnp.maximum(j - 2, 0))
        update(s_ref[1], j - 1, None)
        return t + 1, live()

    lax.while_loop(cond, body, (0, live()))
    acc = acc_ref[...]
    head_row = lax.broadcasted_iota(jnp.int32, (LANES, tq), 0) < HEAD_DIM
    o_t = jnp.where(head_row, acc[:, :tq], acc[:, tq:])
    o_ref[...] = o_t.T.astype(BF16)


def _stick_attention(qkv, *, batch, seq):
    tq, tk = B_TQ, B_TK
    assert tq == 2 * tk
    nq = seq // tq
    n_tiles = B_HEADS * HEAD_DIM // LANES
    kern = functools.partial(_stick_kernel, tq=tq, tk=tk)
    return pl.pallas_call(
        kern,
        out_shape=jax.ShapeDtypeStruct((batch * seq, B_HEADS * HEAD_DIM), BF16),
        grid=(batch, n_tiles, nq),
        in_specs=[pl.BlockSpec((tq, LANES), lambda b, g, i: (b * nq + i, g)),
                  pl.BlockSpec((seq, LANES), lambda b, g, i: (b, n_tiles + g)),
                  pl.BlockSpec((seq, LANES), lambda b, g, i: (b, 2 * n_tiles + g))],
        out_specs=pl.BlockSpec((tq, LANES), lambda b, g, i: (b * nq + i, g)),
        scratch_shapes=[pltpu.VMEM((seq, LANES), BF16),
                        pltpu.VMEM((seq // tk, LANES, tk), BF16),
                        pltpu.VMEM((2, tk, 2 * tq), F32),
                        pltpu.VMEM((1, 2 * tq), F32),
                        pltpu.VMEM((LANES, 2 * tq), F32)],
        compiler_params=_params(("arbitrary", "arbitrary", "arbitrary")),
        name="stick_breaking_attention",
    )(qkv, qkv, qkv)


def _swa_kernel(sink_ref, q_ref, kprev_ref, kcur_ref, vprev_ref, vcur_ref, o_ref, *, tq):
    i = pl.program_id(1)
    w = WINDOW
    qi = lax.broadcasted_iota(jnp.int32, (w, 2 * w), 0)
    ki = lax.broadcasted_iota(jnp.int32, (w, 2 * w), 1)
    rel = w + qi - ki
    band = (rel >= 0) & (rel < w)
    lane = lax.broadcasted_iota(jnp.int32, (w, LANES), 1)
    low = lane < HEAD_DIM
    n_qtiles = C_HEADS * HEAD_DIM // LANES
    tiles_per_group = n_qtiles // C_KV_HEADS
    for sub in range(tq // w):
        rows = slice(sub * w, (sub + 1) * w)
        if sub == 0:
            k_prev, v_prev = kprev_ref[...], vprev_ref[...]
            mask = band & ((ki >= w) | (i > 0))
        else:
            prev = slice((sub - 1) * w, sub * w)
            k_prev, v_prev = kcur_ref[prev, :], vcur_ref[prev, :]
            mask = band
        kwin = jnp.concatenate([k_prev, kcur_ref[rows, :]], axis=0)
        vwin = jnp.concatenate([v_prev, vcur_ref[rows, :]], axis=0)
        for t in range(n_qtiles):
            grp = t // tiles_per_group
            kg = kwin[:, grp * LANES:(grp + 1) * LANES]
            vg = vwin[:, grp * LANES:(grp + 1) * LANES]
            q = q_ref[rows, t * LANES:(t + 1) * LANES]
            zero = jnp.zeros_like(q)
            outs = []
            for c in range(2):
                qc = jnp.where(low, q, zero) if c == 0 else jnp.where(low, zero, q)
                sc = jnp.where(mask, _dot_nt(qc, kg), NEG_BIG)
                sink = sink_ref[2 * t + c] * LOG2E
                mx = jnp.maximum(jnp.max(sc, axis=-1, keepdims=True), sink)
                p = jnp.exp2(sc - mx)
                denom = jnp.sum(p, axis=-1, keepdims=True) + jnp.exp2(sink - mx)
                outs.append(_dot(p.astype(BF16), vg) / denom)
            o_ref[rows, t * LANES:(t + 1) * LANES] = jnp.where(low, outs[0], outs[1]).astype(BF16)


def _swa_attention(qkv, sinks, *, batch, seq):
    tq = C_TQ
    w = WINDOW
    nq = seq // tq
    per = tq // w
    nq_cols = C_HEADS * HEAD_DIM
    kcol = nq_cols // (2 * LANES)
    vcol = kcol + 1

    def prev_map(col):
        return lambda b, i: (jnp.maximum((b * nq + i) * per - 1, 0), col)

    def cur_map(col):
        return lambda b, i: (b * nq + i, col)

    return pl.pallas_call(
        functools.partial(_swa_kernel, tq=tq),
        out_shape=jax.ShapeDtypeStruct((batch * seq, nq_cols), BF16),
        grid=(batch, nq),
        in_specs=[pl.BlockSpec(memory_space=pltpu.SMEM),
                  pl.BlockSpec((tq, nq_cols), lambda b, i: (b * nq + i, 0)),
                  pl.BlockSpec((w, 2 * LANES), prev_map(kcol)),
                  pl.BlockSpec((tq, 2 * LANES), cur_map(kcol)),
                  pl.BlockSpec((w, 2 * LANES), prev_map(vcol)),
                  pl.BlockSpec((tq, 2 * LANES), cur_map(vcol))],
        out_specs=pl.BlockSpec((tq, nq_cols), lambda b, i: (b * nq + i, 0)),
        compiler_params=_params(("arbitrary", "arbitrary")),
        name="sliding_window_attention",
    )(sinks, qkv, qkv, qkv, qkv, qkv)


def kernel(x, positions, norm_gains, a_w_in, a_w_out, a_lambda, a_subln, b_w_in, b_w_out,
           c_w_in, c_w_out, c_sinks, ffn_w_gate, ffn_w_up, ffn_w_down):
    batch, seq, d = x.shape
    depth = norm_gains.shape[0]
    tables = _rope_tables(positions)
    h = x.reshape(batch * seq, d)
    for i in range(depth):
        kind = i % N_MIXERS
        inst = i // N_MIXERS
        g = norm_gains[i]
        if kind == 0:
            qkv = _in_proj(h, g[0], a_w_in[inst].astype(BF16), tables, n_rope=16, n_q=8,
                           q_scale=ATTN_SCALE * LOG2E)
            mix = _diff_attention(qkv, a_lambda[inst], a_subln[inst], batch=batch, seq=seq, layer_idx=i)
            w_out = a_w_out[inst]
        elif kind == 1:
            qkv = _in_proj(h, g[0], b_w_in[inst].astype(BF16), None, n_rope=0, n_q=8,
                           q_scale=ATTN_SCALE * 0.5)
            mix = _stick_attention(qkv, batch=batch, seq=seq)
            w_out = b_w_out[inst]
        else:
            w = c_w_in[inst]
            nq_cols = C_HEADS * HEAD_DIM
            kv = w[:, nq_cols:].reshape(d, 2 * C_KV_HEADS, 1, HEAD_DIM)
            kv = jnp.broadcast_to(kv, (d, 2 * C_KV_HEADS, 2, HEAD_DIM)).reshape(d, 4 * C_KV_HEADS * HEAD_DIM)
            w = jnp.concatenate([w[:, :nq_cols], kv], axis=1).astype(BF16)
            qkv = _in_proj(h, g[0], w, tables, n_rope=10, n_q=8, q_scale=ATTN_SCALE * LOG2E)
            mix = _swa_attention(qkv, c_sinks[inst], batch=batch, seq=seq)
            w_out = c_w_out[inst]
        h = _post_mixer(mix, w_out.astype(BF16), g[1], h, g[2], ffn_w_gate[i].astype(BF16),
                        ffn_w_up[i].astype(BF16), ffn_w_down[i].astype(BF16), g[3])
    return h.reshape(batch, seq, d)
```

```python
import functools
import math

import jax
import jax.numpy as jnp
from jax import lax
from jax.experimental import pallas as pl
from jax.experimental.pallas import tpu as pltpu

F32 = jnp.float32
BF16 = jnp.bfloat16

D_MODEL = 1024
N_MIXERS = 3
ROPE_THETA = 10000.0
ROPE_DIM = 64
NORM_EPS = 1e-6
HEAD_DIM = 64
ATTN_SCALE = HEAD_DIM ** -0.5
A_HEADS = 8
B_HEADS = 16
C_HEADS = 16
C_KV_HEADS = 2
WINDOW = 128

LANES = 128
NEG_BIG = -1e30
NEGLIGIBLE = 2.0 ** -100
VMEM_LIMIT = 56 * 1024 * 1024

ROW_TILE = 1024
A_TQ = 512
A_TK = 256
B_TQ = 512
B_TK = 256
C_TQ = 512
LOG2E = math.log2(math.e)


def _params(sem):
    return pltpu.CompilerParams(dimension_semantics=sem, vmem_limit_bytes=VMEM_LIMIT)


def _rms(x, g):
    ms = jnp.mean(x * x, axis=-1, keepdims=True)
    return x * lax.rsqrt(ms + NORM_EPS) * g


def _dot(a, b):
    return jnp.dot(a, b, preferred_element_type=F32)


def _dot_nt(a, b):
    return lax.dot_general(a, b, (((1,), (1,)), ((), ())), preferred_element_type=F32)


def _halves_side_by_side(q, blk):
    lane = lax.broadcasted_iota(jnp.int32, q.shape, 1)
    zero = jnp.zeros_like(q)
    first = jnp.where(lane < HEAD_DIM, q, zero)
    second = jnp.where(lane >= HEAD_DIM, q, zero)
    parts = []
    for g in range(q.shape[0] // blk):
        rows = slice(g * blk, (g + 1) * blk)
        parts += [first[rows], second[rows]]
    return jnp.concatenate(parts, axis=0)


def _split_side_by_side(x, blk):
    groups = x.shape[1] // (2 * blk)
    first = [x[:, 2 * g * blk:(2 * g + 1) * blk] for g in range(groups)]
    second = [x[:, (2 * g + 1) * blk:(2 * g + 2) * blk] for g in range(groups)]
    return jnp.concatenate(first, axis=1), jnp.concatenate(second, axis=1)


def _rope_table_kernel(pos_ref, inv_ref, sign_ref, cos_ref, sin_ref):
    ang = pos_ref[...].astype(F32) * inv_ref[...]
    cos_ref[...] = jnp.cos(ang)
    sin_ref[...] = jnp.sin(ang) * sign_ref[...]


def _rope_tables(positions):
    m = positions.size
    inv = ROPE_THETA ** (-jnp.arange(0, ROPE_DIM, 2, dtype=F32) / ROPE_DIM)
    inv128 = jnp.tile(inv, LANES // (ROPE_DIM // 2)).reshape(1, LANES)
    half = ROPE_DIM // 2
    sign = jnp.where((jnp.arange(LANES) % ROPE_DIM) < half, -1.0, 1.0).astype(F32).reshape(1, LANES)
    tm = 2048
    return pl.pallas_call(
        _rope_table_kernel,
        out_shape=(jax.ShapeDtypeStruct((m, LANES), F32), jax.ShapeDtypeStruct((m, LANES), F32)),
        grid=(m // tm,),
        in_specs=[pl.BlockSpec((tm, 1), lambda i: (i, 0)),
                  pl.BlockSpec((1, LANES), lambda i: (0, 0)),
                  pl.BlockSpec((1, LANES), lambda i: (0, 0))],
        out_specs=(pl.BlockSpec((tm, LANES), lambda i: (i, 0)),
                   pl.BlockSpec((tm, LANES), lambda i: (i, 0))),
        compiler_params=_params(("arbitrary",)),
        name="rope_tables",
    )(positions.reshape(m, 1), inv128, sign)


def _swap_halves(x):
    lane = lax.broadcasted_iota(jnp.int32, x.shape, 1)
    fwd = pltpu.roll(x, ROPE_DIM // 2, 1)
    bwd = pltpu.roll(x, LANES - ROPE_DIM // 2, 1)
    return jnp.where((lane & (ROPE_DIM - 1)) < ROPE_DIM // 2, bwd, fwd)


def _in_proj_kernel(*refs, n_tiles, n_rope, n_q, q_scale, chunk):
    if n_rope:
        x_ref, g_ref, w_ref, cos_ref, sin_ref, o_ref = refs
    else:
        x_ref, g_ref, w_ref, o_ref = refs
    hn = _rms(x_ref[...], g_ref[...]).astype(BF16)
    if n_rope:
        cos = cos_ref[...]
        sin = sin_ref[...]
    tiles_per_chunk = chunk // LANES
    for c in range(n_tiles // tiles_per_chunk):
        acc = _dot(hn, w_ref[:, c * chunk:(c + 1) * chunk])
        for t in range(tiles_per_chunk):
            tile = c * tiles_per_chunk + t
            a = acc[:, t * LANES:(t + 1) * LANES]
            if tile < n_rope:
                a = a * cos + _swap_halves(a) * sin
            if tile < n_q:
                a = a * q_scale
            o_ref[:, tile * LANES:(tile + 1) * LANES] = a.astype(BF16)


def _in_proj(x, g, w, tables, *, n_rope, n_q, q_scale):
    m = x.shape[0]
    n = w.shape[1]
    tm = ROW_TILE
    chunk = 256
    kern = functools.partial(_in_proj_kernel, n_tiles=n // LANES, n_rope=n_rope, n_q=n_q,
                             q_scale=q_scale, chunk=chunk)
    in_specs = [pl.BlockSpec((tm, D_MODEL), lambda i: (i, 0)),
                pl.BlockSpec((1, D_MODEL), lambda i: (0, 0)),
                pl.BlockSpec((D_MODEL, n), lambda i: (0, 0))]
    args = [x, g.reshape(1, D_MODEL), w]
    if n_rope:
        in_specs += [pl.BlockSpec((tm, LANES), lambda i: (i, 0)),
                     pl.BlockSpec((tm, LANES), lambda i: (i, 0))]
        args += list(tables)
    return pl.pallas_call(
        kern,
        out_shape=jax.ShapeDtypeStruct((m, n), BF16),
        grid=(m // tm,),
        in_specs=in_specs,
        out_specs=pl.BlockSpec((tm, n), lambda i: (i, 0)),
        compiler_params=_params(("arbitrary",)),
        name="in_proj",
    )(*args)


def _post_mixer_kernel(mix_ref, wo_ref, gmix_ref, x_ref, gin_ref, wg_ref, wu_ref, wd_ref, gout_ref,
                       o_ref, act_ref, *, chunk):
    x = x_ref[...] + _rms(_dot(mix_ref[...], wo_ref[...]), gmix_ref[...])
    hn = _rms(x, gin_ref[...]).astype(BF16)
    hidden = wg_ref.shape[1]
    for c in range(hidden // chunk):
        cols = slice(c * chunk, (c + 1) * chunk)
        gate = _dot(hn, wg_ref[:, cols])
        up = _dot(hn, wu_ref[:, cols])
        act_ref[:, cols] = (gate * jax.nn.sigmoid(gate) * up).astype(BF16)
    f = _dot(act_ref[...], wd_ref[...])
    o_ref[...] = x + _rms(f, gout_ref[...])


def _post_mixer(mix, wo, gmix, x, gin, wg, wu, wd, gout):
    m = x.shape[0]
    hidden = wg.shape[1]
    tm = ROW_TILE
    const = lambda i: (0, 0)
    rows = lambda i: (i, 0)
    resident = functools.partial(pl.BlockSpec, index_map=const, pipeline_mode=pl.Buffered(1))
    gain = pl.BlockSpec((1, D_MODEL), const)
    return pl.pallas_call(
        functools.partial(_post_mixer_kernel, chunk=256),
        out_shape=jax.ShapeDtypeStruct((m, D_MODEL), F32),
        grid=(m // tm,),
        in_specs=[pl.BlockSpec((tm, D_MODEL), rows),
                  resident((D_MODEL, D_MODEL)),
                  gain,
                  pl.BlockSpec((tm, D_MODEL), rows),
                  gain,
                  resident((D_MODEL, hidden)),
                  resident((D_MODEL, hidden)),
                  resident((hidden, D_MODEL)),
                  gain],
        out_specs=pl.BlockSpec((tm, D_MODEL), rows),
        scratch_shapes=[pltpu.VMEM((tm, hidden), BF16)],
        compiler_params=_params(("arbitrary",)),
        name="out_proj_ffn",
    )(mix, wo, gmix.reshape(1, D_MODEL), x, gin.reshape(1, D_MODEL), wg, wu, wd,
      gout.reshape(1, D_MODEL))


def _diff_attn_kernel(lam_ref, subln_ref, q_ref, k_ref, v_ref, o_ref,
                      vt_ref, s_ref, m_ref, l_ref, acc_ref, *, tq, tk, lam_init):
    i = pl.program_id(2)
    n = 2 * tq

    @pl.when(i == 0)
    def _():
        for c in range(vt_ref.shape[0]):
            vt_ref[c] = v_ref[c * tk:(c + 1) * tk, :].astype(F32).T.astype(BF16)

    qcat = _halves_side_by_side(q_ref[...], tk)

    m_ref[...] = jnp.full(m_ref.shape, NEG_BIG, F32)
    l_ref[...] = jnp.zeros(l_ref.shape, F32)
    acc_ref[...] = jnp.zeros(acc_ref.shape, F32)

    def scores(j, lo=0):
        kb = k_ref[pl.ds(pl.multiple_of(j * tk, tk), tk), :]
        return _dot_nt(kb, qcat[lo:])

    def update(s, j, lo=0):
        m_old = m_ref[:, lo:]
        m_new = jnp.maximum(m_old, jnp.max(s, axis=0, keepdims=True))
        alpha = jnp.exp2(m_old - m_new)
        p = jnp.exp2(s - m_new)
        l_ref[:, lo:] = alpha * l_ref[:, lo:] + jnp.sum(p, axis=0, keepdims=True)
        acc_ref[:, lo:] = alpha * acc_ref[:, lo:] + _dot(vt_ref[j], p.astype(BF16))
        m_ref[:, lo:] = m_new

    s_ref[0] = scores(0)

    def pair(j):
        s_ref[1] = scores(j + 1)
        update(s_ref[0], j)
        s_ref[0] = scores(j + 2)
        update(s_ref[1], j + 1)

    def body(t, carry):
        pair(4 * t)
        pair(4 * t + 2)
        return carry

    lax.fori_loop(0, i >> 1, body, 0)

    @pl.when((i & 1) == 1)
    def _():
        pair(2 * i - 2)

    j = 2 * i
    s_ref[1, :, tq:] = scores(j + 1, tq)
    key = lax.broadcasted_iota(jnp.int32, (tk, n), 0)
    col = lax.broadcasted_iota(jnp.int32, (tk, n), 1)
    causal = (key <= (col & (tk - 1))) | (col >= tq)
    update(jnp.where(causal, s_ref[0], NEG_BIG), j)
    update(jnp.where(causal[:, :tq], s_ref[1, :, tq:], NEG_BIG), j + 1, tq)

    lp = lam_ref[...]
    lam = (jnp.exp(jnp.sum(lp[0:1] * lp[1:2], axis=-1, keepdims=True))
           - jnp.exp(jnp.sum(lp[2:3] * lp[3:4], axis=-1, keepdims=True)) + lam_init)
    first, second = _split_side_by_side(acc_ref[...] / l_ref[...], tk)
    o_t = first - lam * second
    o = _rms(o_t.T, subln_ref[...]) * (1.0 - lam_init)
    o_ref[...] = o.astype(BF16)


def _diff_attention(qkv, lam_params, subln_g, *, batch, seq, layer_idx):
    tq, tk = A_TQ, A_TK
    assert tq == 2 * tk
    nq = seq // tq
    lam_init = 0.8 - 0.6 * math.exp(-0.3 * layer_idx)
    kern = functools.partial(_diff_attn_kernel, tq=tq, tk=tk, lam_init=lam_init)
    return pl.pallas_call(
        kern,
        out_shape=jax.ShapeDtypeStruct((batch * seq, A_HEADS * LANES), BF16),
        grid=(batch, A_HEADS, nq),
        in_specs=[pl.BlockSpec((4, HEAD_DIM), lambda b, h, i: (0, 0)),
                  pl.BlockSpec((1, LANES), lambda b, h, i: (0, 0)),
                  pl.BlockSpec((tq, LANES), lambda b, h, i: (b * nq + i, h)),
                  pl.BlockSpec((seq, LANES), lambda b, h, i: (b, A_HEADS + h)),
                  pl.BlockSpec((seq, LANES), lambda b, h, i: (b, 2 * A_HEADS + h))],
        out_specs=pl.BlockSpec((tq, LANES), lambda b, h, i: (b * nq + i, h)),
        scratch_shapes=[pltpu.VMEM((seq // tk, LANES, tk), BF16),
                        pltpu.VMEM((2, tk, 2 * tq), F32),
                        pltpu.VMEM((1, 2 * tq), F32), pltpu.VMEM((1, 2 * tq), F32),
                        pltpu.VMEM((LANES, 2 * tq), F32)],
        compiler_params=_params(("arbitrary", "arbitrary", "arbitrary")),
        name="diff_attention",
    )(lam_params, subln_g.reshape(1, LANES), qkv, qkv, qkv)


def _stick_kernel(q_ref, k_ref, v_ref, o_ref, kp_ref, vt_ref, s_ref, carry_ref, acc_ref, *, tq, tk):
    i = pl.program_id(2)
    n = 2 * tq
    nv = tk // 8

    @pl.when(i == 0)
    def _():
        r = lax.broadcasted_iota(jnp.int32, (tk, tk), 0)
        c = lax.broadcasted_iota(jnp.int32, (tk, tk), 1)
        perm = jnp.where(c == (r & 7) * nv + (r >> 3), 1.0, 0.0).astype(BF16)

        for blk in range(vt_ref.shape[0]):
            rows = slice(blk * tk, (blk + 1) * tk)
            kv = _dot(perm, jnp.concatenate([k_ref[rows, :], v_ref[rows, :]], axis=1))
            kp_ref[rows, :] = kv[:, :LANES].astype(BF16)
            vt_ref[blk] = kv[:, LANES:].T.astype(BF16)

    qcat = _halves_side_by_side(q_ref[...], tk)

    carry_ref[...] = jnp.ones(carry_ref.shape, F32)
    acc_ref[...] = jnp.zeros(acc_ref.shape, F32)

    def scores(j, lo=0):
        kb = kp_ref[pl.ds(pl.multiple_of(j * tk, tk), tk), :]
        return _dot_nt(kb, qcat[lo:])

    row = lax.broadcasted_iota(jnp.int32, (tk, n), 0)
    col = lax.broadcasted_iota(jnp.int32, (tk, n), 1)
    kpos = (row & 7) * nv + (row >> 3)
    strict = (kpos < (col & (tk - 1))) | (col >= tq)

    def update(zh, j, lo=0):
        w = n - lo
        half_tanh = 0.5 * jnp.tanh(zh)
        beta = 0.5 + half_tanh
        omb = 0.5 - half_tanh
        run = jnp.ones((8, w), F32)
        bw = [None] * nv
        for v in reversed(range(nv)):
            rows = slice(v * 8, (v + 1) * 8)
            bw[v] = beta[rows] * run
            run = run * omb[rows]
        sub = lax.broadcasted_iota(jnp.int32, (8, w), 0)
        later = jnp.ones((8, w), F32)
        for sp in range(7, 0, -1):
            later = jnp.where(sub < sp, later * run[sp:sp + 1, :], later)
        carry = carry_ref[:, lo:]
        scale = later * carry
        a = jnp.concatenate([bw[v] * scale for v in range(nv)], axis=0).astype(BF16)
        acc_ref[:, lo:] += _dot(vt_ref[j], a)
        carry_ref[:, lo:] = carry * (later[0:1, :] * run[0:1, :])

    top = 2 * i + 1
    s_ref[0, :, tq:] = scores(top, tq)
    s_ref[1] = scores(top - 1)
    update(jnp.where(strict[:, :tq], s_ref[0, :, tq:], NEG_BIG), top, tq)
    s_ref[0] = scores(jnp.maximum(top - 2, 0))
    update(jnp.where(strict, s_ref[1], NEG_BIG), top - 1)

    def live():
        return jnp.max(carry_ref[...]) > NEGLIGIBLE

    def cond(state):
        t, alive = state
        return jnp.logical_and(t < i, alive)

    def body(state):
        t, _ = state
        j = top - 2 - 2 * t
        s_ref[1] = scores(j - 1)
        update(s_ref[0], j)
        s_ref[0] = scores(jnp.maximum(j - 2, 0))
        update(s_ref[1], j - 1)
        return t + 1, live()

    lax.while_loop(cond, body, (0, live()))
    even_head, odd_head = _split_side_by_side(acc_ref[...], tk)
    head_row = lax.broadcasted_iota(jnp.int32, (LANES, tq), 0) < HEAD_DIM
    o_t = jnp.where(head_row, even_head, odd_head)
    o_ref[...] = o_t.T.astype(BF16)


def _stick_attention(qkv, *, batch, seq):
    tq, tk = B_TQ, B_TK
    assert tq == 2 * tk
    nq = seq // tq
    n_tiles = B_HEADS * HEAD_DIM // LANES
    kern = functools.partial(_stick_kernel, tq=tq, tk=tk)
    return pl.pallas_call(
        kern,
        out_shape=jax.ShapeDtypeStruct((batch * seq, B_HEADS * HEAD_DIM), BF16),
        grid=(batch, n_tiles, nq),
        in_specs=[pl.BlockSpec((tq, LANES), lambda b, g, i: (b * nq + i, g)),
                  pl.BlockSpec((seq, LANES), lambda b, g, i: (b, n_tiles + g)),
                  pl.BlockSpec((seq, LANES), lambda b, g, i: (b, 2 * n_tiles + g))],
        out_specs=pl.BlockSpec((tq, LANES), lambda b, g, i: (b * nq + i, g)),
        scratch_shapes=[pltpu.VMEM((seq, LANES), BF16),
                        pltpu.VMEM((seq // tk, LANES, tk), BF16),
                        pltpu.VMEM((2, tk, 2 * tq), F32),
                        pltpu.VMEM((1, 2 * tq), F32),
                        pltpu.VMEM((LANES, 2 * tq), F32)],
        compiler_params=_params(("arbitrary", "arbitrary", "arbitrary")),
        name="stick_breaking_attention",
    )(qkv, qkv, qkv)


def _swa_kernel(sink_ref, q_ref, kprev_ref, kcur_ref, vprev_ref, vcur_ref, o_ref, *, tq):
    i = pl.program_id(1)
    w = WINDOW
    qi = lax.broadcasted_iota(jnp.int32, (w, 2 * w), 0)
    ki = lax.broadcasted_iota(jnp.int32, (w, 2 * w), 1)
    rel = w + qi - ki
    band = (rel >= 0) & (rel < w)
    lane = lax.broadcasted_iota(jnp.int32, (w, LANES), 1)
    low = lane < HEAD_DIM
    n_qtiles = C_HEADS * HEAD_DIM // LANES
    tiles_per_group = n_qtiles // C_KV_HEADS
    for sub in range(tq // w):
        rows = slice(sub * w, (sub + 1) * w)
        if sub == 0:
            k_prev, v_prev = kprev_ref[...], vprev_ref[...]
            mask = band & ((ki >= w) | (i > 0))
        else:
            prev = slice((sub - 1) * w, sub * w)
            k_prev, v_prev = kcur_ref[prev, :], vcur_ref[prev, :]
            mask = band
        kwin = jnp.concatenate([k_prev, kcur_ref[rows, :]], axis=0)
        vwin = jnp.concatenate([v_prev, vcur_ref[rows, :]], axis=0)
        for t in range(n_qtiles):
            grp = t // tiles_per_group
            kg = kwin[:, grp * LANES:(grp + 1) * LANES]
            vg = vwin[:, grp * LANES:(grp + 1) * LANES]
            q = q_ref[rows, t * LANES:(t + 1) * LANES]
            zero = jnp.zeros_like(q)
            outs = []
            for c in range(2):
                qc = jnp.where(low, q, zero) if c == 0 else jnp.where(low, zero, q)
                sc = jnp.where(mask, _dot_nt(qc, kg), NEG_BIG)
                sink = sink_ref[2 * t + c] * LOG2E
                mx = jnp.maximum(jnp.max(sc, axis=-1, keepdims=True), sink)
                p = jnp.exp2(sc - mx)
                denom = jnp.sum(p, axis=-1, keepdims=True) + jnp.exp2(sink - mx)
                outs.append(_dot(p.astype(BF16), vg) / denom)
            o_ref[rows, t * LANES:(t + 1) * LANES] = jnp.where(low, outs[0], outs[1]).astype(BF16)


def _swa_attention(qkv, sinks, *, batch, seq):
    tq = C_TQ
    w = WINDOW
    nq = seq // tq
    per = tq // w
    nq_cols = C_HEADS * HEAD_DIM
    kcol = nq_cols // (2 * LANES)
    vcol = kcol + 1

    def prev_map(col):
        return lambda b, i: (jnp.maximum((b * nq + i) * per - 1, 0), col)

    def cur_map(col):
        return lambda b, i: (b * nq + i, col)

    return pl.pallas_call(
        functools.partial(_swa_kernel, tq=tq),
        out_shape=jax.ShapeDtypeStruct((batch * seq, nq_cols), BF16),
        grid=(batch, nq),
        in_specs=[pl.BlockSpec(memory_space=pltpu.SMEM),
                  pl.BlockSpec((tq, nq_cols), lambda b, i: (b * nq + i, 0)),
                  pl.BlockSpec((w, 2 * LANES), prev_map(kcol)),
                  pl.BlockSpec((tq, 2 * LANES), cur_map(kcol)),
                  pl.BlockSpec((w, 2 * LANES), prev_map(vcol)),
                  pl.BlockSpec((tq, 2 * LANES), cur_map(vcol))],
        out_specs=pl.BlockSpec((tq, nq_cols), lambda b, i: (b * nq + i, 0)),
        compiler_params=_params(("arbitrary", "arbitrary")),
        name="sliding_window_attention",
    )(sinks, qkv, qkv, qkv, qkv, qkv)


def kernel(x, positions, norm_gains, a_w_in, a_w_out, a_lambda, a_subln, b_w_in, b_w_out,
           c_w_in, c_w_out, c_sinks, ffn_w_gate, ffn_w_up, ffn_w_down):
    batch, seq, d = x.shape
    depth = norm_gains.shape[0]
    tables = _rope_tables(positions)
    h = x.reshape(batch * seq, d)
    for i in range(depth):
        kind = i % N_MIXERS
        inst = i // N_MIXERS
        g = norm_gains[i]
        if kind == 0:
            qkv = _in_proj(h, g[0], a_w_in[inst].astype(BF16), tables, n_rope=16, n_q=8,
                           q_scale=ATTN_SCALE * LOG2E)
            mix = _diff_attention(qkv, a_lambda[inst], a_subln[inst], batch=batch, seq=seq, layer_idx=i)
            w_out = a_w_out[inst]
        elif kind == 1:
            qkv = _in_proj(h, g[0], b_w_in[inst].astype(BF16), None, n_rope=0, n_q=8,
                           q_scale=ATTN_SCALE * 0.5)
            mix = _stick_attention(qkv, batch=batch, seq=seq)
            w_out = b_w_out[inst]
        else:
            w = c_w_in[inst]
            nq_cols = C_HEADS * HEAD_DIM
            kv = w[:, nq_cols:].reshape(d, 2 * C_KV_HEADS, 1, HEAD_DIM)
            kv = jnp.broadcast_to(kv, (d, 2 * C_KV_HEADS, 2, HEAD_DIM)).reshape(d, 4 * C_KV_HEADS * HEAD_DIM)
            w = jnp.concatenate([w[:, :nq_cols], kv], axis=1).astype(BF16)
            qkv = _in_proj(h, g[0], w, tables, n_rope=10, n_q=8, q_scale=ATTN_SCALE * LOG2E)
            mix = _swa_attention(qkv, c_sinks[inst], batch=batch, seq=seq)
            w_out = c_w_out[inst]
        h = _post_mixer(mix, w_out.astype(BF16), g[1], h, g[2], ffn_w_gate[i].astype(BF16),
                        ffn_w_up[i].astype(BF16), ffn_w_down[i].astype(BF16), g[3])
    return h.reshape(batch, seq, d)
```

```python
import functools
import math

import jax
import jax.numpy as jnp
from jax import lax
from jax.experimental import pallas as pl
from jax.experimental.pallas import tpu as pltpu

F32 = jnp.float32
BF16 = jnp.bfloat16

D_MODEL = 1024
N_MIXERS = 3
ROPE_THETA = 10000.0
ROPE_DIM = 64
NORM_EPS = 1e-6
HEAD_DIM = 64
ATTN_SCALE = HEAD_DIM ** -0.5
A_HEADS = 8
B_HEADS = 16
C_HEADS = 16
C_KV_HEADS = 2
WINDOW = 128

LANES = 128
NEG_BIG = -1e30
NEGLIGIBLE = 2.0 ** -100
VMEM_LIMIT = 56 * 1024 * 1024

ROW_TILE = 1024
A_TQ = 512
A_TK = 256
B_TQ = 512
B_TK = 256
C_TQ = 512
LOG2E = math.log2(math.e)
SUM_ROWS = 16

def _params(sem):
    return pltpu.CompilerParams(dimension_semantics=sem, vmem_limit_bytes=VMEM_LIMIT)


def _rms(x, g):
    ms = jnp.mean(x * x, axis=-1, keepdims=True)
    return x * lax.rsqrt(ms + NORM_EPS) * g


def _dot(a, b):
    return jnp.dot(a, b, preferred_element_type=F32)


def _dot_nt(a, b):
    return lax.dot_general(a, b, (((1,), (1,)), ((), ())), preferred_element_type=F32)


def _halves_side_by_side(q, blk):
    lane = lax.broadcasted_iota(jnp.int32, q.shape, 1)
    zero = jnp.zeros_like(q)
    first = jnp.where(lane < HEAD_DIM, q, zero)
    second = jnp.where(lane >= HEAD_DIM, q, zero)
    parts = []
    for g in range(q.shape[0] // blk):
        rows = slice(g * blk, (g + 1) * blk)
        parts += [first[rows], second[rows]]
    return jnp.concatenate(parts, axis=0)


def _split_side_by_side(x, blk):
    groups = x.shape[1] // (2 * blk)
    first = [x[:, 2 * g * blk:(2 * g + 1) * blk] for g in range(groups)]
    second = [x[:, (2 * g + 1) * blk:(2 * g + 2) * blk] for g in range(groups)]
    return jnp.concatenate(first, axis=1), jnp.concatenate(second, axis=1)


def _rope_table_kernel(pos_ref, inv_ref, sign_ref, cos_ref, sin_ref):
    ang = pos_ref[...].astype(F32) * inv_ref[...]
    cos_ref[...] = jnp.cos(ang)
    sin_ref[...] = jnp.sin(ang) * sign_ref[...]


def _rope_tables(positions):
    m = positions.size
    inv = ROPE_THETA ** (-jnp.arange(0, ROPE_DIM, 2, dtype=F32) / ROPE_DIM)
    inv128 = jnp.tile(inv, LANES // (ROPE_DIM // 2)).reshape(1, LANES)
    half = ROPE_DIM // 2
    sign = jnp.where((jnp.arange(LANES) % ROPE_DIM) < half, -1.0, 1.0).astype(F32).reshape(1, LANES)
    tm = 2048
    return pl.pallas_call(
        _rope_table_kernel,
        out_shape=(jax.ShapeDtypeStruct((m, LANES), F32), jax.ShapeDtypeStruct((m, LANES), F32)),
        grid=(m // tm,),
        in_specs=[pl.BlockSpec((tm, 1), lambda i: (i, 0)),
                  pl.BlockSpec((1, LANES), lambda i: (0, 0)),
                  pl.BlockSpec((1, LANES), lambda i: (0, 0))],
        out_specs=(pl.BlockSpec((tm, LANES), lambda i: (i, 0)),
                   pl.BlockSpec((tm, LANES), lambda i: (i, 0))),
        compiler_params=_params(("arbitrary",)),
        name="rope_tables",
    )(positions.reshape(m, 1), inv128, sign)


def _swap_halves(x):
    lane = lax.broadcasted_iota(jnp.int32, x.shape, 1)
    fwd = pltpu.roll(x, ROPE_DIM // 2, 1)
    bwd = pltpu.roll(x, LANES - ROPE_DIM // 2, 1)
    return jnp.where((lane & (ROPE_DIM - 1)) < ROPE_DIM // 2, bwd, fwd)


def _in_proj_kernel(*refs, n_tiles, n_rope, n_q, q_scale, chunk):
    if n_rope:
        x_ref, g_ref, w_ref, cos_ref, sin_ref, o_ref = refs
    else:
        x_ref, g_ref, w_ref, o_ref = refs
    hn = _rms(x_ref[...], g_ref[...]).astype(BF16)
    if n_rope:
        cos = cos_ref[...]
        sin = sin_ref[...]
    tiles_per_chunk = chunk // LANES
    for c in range(n_tiles // tiles_per_chunk):
        acc = _dot(hn, w_ref[:, c * chunk:(c + 1) * chunk])
        for t in range(tiles_per_chunk):
            tile = c * tiles_per_chunk + t
            a = acc[:, t * LANES:(t + 1) * LANES]
            if tile < n_rope:
                a = a * cos + _swap_halves(a) * sin
            if tile < n_q:
                a = a * q_scale
            o_ref[:, tile * LANES:(tile + 1) * LANES] = a.astype(BF16)


def _in_proj(x, g, w, tables, *, n_rope, n_q, q_scale):
    m = x.shape[0]
    n = w.shape[1]
    tm = ROW_TILE
    chunk = 256
    kern = functools.partial(_in_proj_kernel, n_tiles=n // LANES, n_rope=n_rope, n_q=n_q,
                             q_scale=q_scale, chunk=chunk)
    in_specs = [pl.BlockSpec((tm, D_MODEL), lambda i: (i, 0)),
                pl.BlockSpec((1, D_MODEL), lambda i: (0, 0)),
                pl.BlockSpec((D_MODEL, n), lambda i: (0, 0))]
    args = [x, g.reshape(1, D_MODEL), w]
    if n_rope:
        in_specs += [pl.BlockSpec((tm, LANES), lambda i: (i, 0)),
                     pl.BlockSpec((tm, LANES), lambda i: (i, 0))]
        args += list(tables)
    return pl.pallas_call(
        kern,
        out_shape=jax.ShapeDtypeStruct((m, n), BF16),
        grid=(m // tm,),
        in_specs=in_specs,
        out_specs=pl.BlockSpec((tm, n), lambda i: (i, 0)),
        compiler_params=_params(("arbitrary",)),
        name="in_proj",
    )(*args)


def _post_mixer_kernel(mix_ref, wo_ref, gmix_ref, x_ref, gin_ref, wg_ref, wu_ref, wd_ref, gout_ref,
                       o_ref, act_ref, *, chunk):
    x = x_ref[...] + _rms(_dot(mix_ref[...], wo_ref[...]), gmix_ref[...])
    hn = _rms(x, gin_ref[...]).astype(BF16)
    hidden = wg_ref.shape[1]
    for c in range(hidden // chunk):
        cols = slice(c * chunk, (c + 1) * chunk)
        gate = _dot(hn, wg_ref[:, cols])
        up = _dot(hn, wu_ref[:, cols])
        act_ref[:, cols] = (gate * jax.nn.sigmoid(gate) * up).astype(BF16)
    f = _dot(act_ref[...], wd_ref[...])
    o_ref[...] = x + _rms(f, gout_ref[...])


def _post_mixer(mix, wo, gmix, x, gin, wg, wu, wd, gout):
    m = x.shape[0]
    hidden = wg.shape[1]
    tm = ROW_TILE
    const = lambda i: (0, 0)
    rows = lambda i: (i, 0)
    resident = functools.partial(pl.BlockSpec, index_map=const, pipeline_mode=pl.Buffered(1))
    gain = pl.BlockSpec((1, D_MODEL), const)
    return pl.pallas_call(
        functools.partial(_post_mixer_kernel, chunk=256),
        out_shape=jax.ShapeDtypeStruct((m, D_MODEL), F32),
        grid=(m // tm,),
        in_specs=[pl.BlockSpec((tm, D_MODEL), rows),
                  resident((D_MODEL, D_MODEL)),
                  gain,
                  pl.BlockSpec((tm, D_MODEL), rows),
                  gain,
                  resident((D_MODEL, hidden)),
                  resident((D_MODEL, hidden)),
                  resident((hidden, D_MODEL)),
                  gain],
        out_specs=pl.BlockSpec((tm, D_MODEL), rows),
        scratch_shapes=[pltpu.VMEM((tm, hidden), BF16)],
        compiler_params=_params(("arbitrary",)),
        name="out_proj_ffn",
    )(mix, wo, gmix.reshape(1, D_MODEL), x, gin.reshape(1, D_MODEL), wg, wu, wd,
      gout.reshape(1, D_MODEL))


def _diff_attn_kernel(lam_ref, subln_ref, q_ref, k_ref, v_ref, o_ref,
                      vt_ref, tri_ref, s_ref, m_ref, acc_ref, *, tq, tk, lam_init):
    i = pl.program_id(2)
    n = 2 * tq

    @pl.when(i == 0)
    def _():
        ones_row = jnp.where(lax.broadcasted_iota(jnp.int32, (SUM_ROWS, tk), 0) == 0, 1.0, 0.0)
        for c in range(vt_ref.shape[0]):
            vt_ref[c, 0:LANES, :] = v_ref[c * tk:(c + 1) * tk, :].astype(F32).T.astype(BF16)
            vt_ref[c, LANES:, :] = ones_row.astype(BF16)
        key = lax.broadcasted_iota(jnp.int32, (tk, tq), 0)
        qry = lax.broadcasted_iota(jnp.int32, (tk, tq), 1) & (tk - 1)
        tri_ref[...] = jnp.where(key <= qry, 0.0, NEG_BIG)

    def query_columns(step):
        q = q_ref[pl.ds(pl.multiple_of(step * tq, tq), tq), :]
        return _halves_side_by_side(q, tk)

    qcat = query_columns(i)

    m_ref[...] = jnp.full(m_ref.shape, NEG_BIG, F32)
    acc_ref[...] = jnp.zeros(acc_ref.shape, F32)

    def scores(j, lo=0):
        kb = k_ref[pl.ds(pl.multiple_of(j * tk, tk), tk), :]
        return _dot_nt(kb, qcat[lo:])

    @pl.when(i == 0)
    def _():
        s_ref[0] = scores(0)

    def update(s, j, lo=0):
        m_old = m_ref[:, lo:]
        m_new = jnp.maximum(m_old, jnp.max(s, axis=0, keepdims=True))
        alpha = jnp.exp2(m_old - m_new)
        p = jnp.exp2(s - m_new).astype(BF16)
        acc_ref[:, lo:] = alpha * acc_ref[:, lo:] + _dot(vt_ref[j], p)
        m_ref[:, lo:] = m_new

    def pair(j):
        s_ref[1] = scores(j + 1)
        update(s_ref[0], j)
        s_ref[0] = scores(j + 2)
        update(s_ref[1], j + 1)

    def body(t, carry):
        pair(4 * t)
        pair(4 * t + 2)
        return carry

    lax.fori_loop(0, i >> 1, body, 0)

    @pl.when((i & 1) == 1)
    def _():
        pair(2 * i - 2)

    j = 2 * i
    s_ref[1, :, tq:] = scores(j + 1, tq)
    tri = tri_ref[...]
    update(jnp.concatenate([s_ref[0, :, :tq] + tri, s_ref[0, :, tq:]], axis=1), j)
    update(s_ref[1, :, tq:] + tri, j + 1, tq)

    nxt = jnp.minimum(i + 1, pl.num_programs(2) - 1)
    s_ref[0] = _dot_nt(k_ref[0:tk, :], query_columns(nxt))

    lp = lam_ref[...]
    lam = (jnp.exp(jnp.sum(lp[0:1] * lp[1:2], axis=-1, keepdims=True))
           - jnp.exp(jnp.sum(lp[2:3] * lp[3:4], axis=-1, keepdims=True)) + lam_init)
    acc = acc_ref[...]
    first, second = _split_side_by_side(acc[:LANES] / acc[LANES:LANES + 1], tk)
    o_t = first - lam * second
    ms = jnp.mean(o_t * o_t, axis=0, keepdims=True)
    o_t = o_t * lax.rsqrt(ms + NORM_EPS) * (subln_ref[...] * (1.0 - lam_init))
    o_ref[...] = o_t.T.astype(BF16)


def _diff_attention(qkv, lam_params, subln_g, *, batch, seq, layer_idx):
    tq, tk = A_TQ, A_TK
    assert tq == 2 * tk
    nq = seq // tq
    lam_init = 0.8 - 0.6 * math.exp(-0.3 * layer_idx)
    kern = functools.partial(_diff_attn_kernel, tq=tq, tk=tk, lam_init=lam_init)
    return pl.pallas_call(
        kern,
        out_shape=jax.ShapeDtypeStruct((batch * seq, A_HEADS * LANES), BF16),
        grid=(batch, A_HEADS, nq),
        in_specs=[pl.BlockSpec((4, HEAD_DIM), lambda b, h, i: (0, 0)),
                  pl.BlockSpec((LANES, 1), lambda b, h, i: (0, 0)),
                  pl.BlockSpec((seq, LANES), lambda b, h, i: (b, h)),
                  pl.BlockSpec((seq, LANES), lambda b, h, i: (b, A_HEADS + h)),
                  pl.BlockSpec((seq, LANES), lambda b, h, i: (b, 2 * A_HEADS + h))],
        out_specs=pl.BlockSpec((tq, LANES), lambda b, h, i: (b * nq + i, h)),
        scratch_shapes=[pltpu.VMEM((seq // tk, LANES + SUM_ROWS, tk), BF16),
                        pltpu.VMEM((tk, tq), F32),
                        pltpu.VMEM((2, tk, 2 * tq), F32),
                        pltpu.VMEM((1, 2 * tq), F32),
                        pltpu.VMEM((LANES + SUM_ROWS, 2 * tq), F32)],
        compiler_params=_params(("arbitrary", "arbitrary", "arbitrary")),
        name="diff_attention",
    )(lam_params, subln_g.reshape(LANES, 1), qkv, qkv, qkv)


def _stick_kernel(q_ref, k_ref, v_ref, o_ref, kp_ref, vt_ref, tri_ref, s_ref, carry_ref, acc_ref,
                  *, tq, tk):
    i = pl.program_id(2)
    n = 2 * tq
    nv = tk // 8

    @pl.when(i == 0)
    def _():
        r = lax.broadcasted_iota(jnp.int32, (tk, tk), 0)
        c = lax.broadcasted_iota(jnp.int32, (tk, tk), 1)
        perm = jnp.where(c == (r & 7) * nv + (r >> 3), 1.0, 0.0).astype(BF16)

        for blk in range(vt_ref.shape[0]):
            rows = slice(blk * tk, (blk + 1) * tk)
            kv = _dot(perm, jnp.concatenate([k_ref[rows, :], v_ref[rows, :]], axis=1))
            kp_ref[rows, :] = kv[:, :LANES].astype(BF16)
            vt_ref[blk] = kv[:, LANES:].T.astype(BF16)
        row = lax.broadcasted_iota(jnp.int32, (tk, tq), 0)
        kpos = (row & 7) * nv + (row >> 3)
        qry = lax.broadcasted_iota(jnp.int32, (tk, tq), 1) & (tk - 1)
        tri_ref[...] = jnp.where(kpos < qry, 0.0, NEG_BIG)

    qcat = _halves_side_by_side(q_ref[...], tk)

    carry_ref[...] = jnp.ones(carry_ref.shape, F32)
    acc_ref[...] = jnp.zeros(acc_ref.shape, F32)

    def scores(j, lo=0):
        kb = kp_ref[pl.ds(pl.multiple_of(j * tk, tk), tk), :]
        return _dot_nt(kb, qcat[lo:])


    def update(zh, j, lo=0):
        w = n - lo
        half_tanh = 0.5 * jnp.tanh(zh)
        beta = 0.5 + half_tanh
        omb = 0.5 - half_tanh
        run = jnp.ones((8, w), F32)
        bw = [None] * nv
        for v in reversed(range(nv)):
            rows = slice(v * 8, (v + 1) * 8)
            bw[v] = beta[rows] * run
            run = run * omb[rows]
        sub = lax.broadcasted_iota(jnp.int32, (8, w), 0)
        later = jnp.ones((8, w), F32)
        for sp in range(7, 0, -1):
            later = jnp.where(sub < sp, later * run[sp:sp + 1, :], later)
        carry = carry_ref[:, lo:]
        scale = later * carry
        a = jnp.concatenate([bw[v] * scale for v in range(nv)], axis=0).astype(BF16)
        acc_ref[:, lo:] += _dot(vt_ref[j], a)
        carry_ref[:, lo:] = carry * (later[0:1, :] * run[0:1, :])

    top = 2 * i + 1
    s_ref[0, :, tq:] = scores(top, tq)
    s_ref[1] = scores(top - 1)
    tri = tri_ref[...]
    update(s_ref[0, :, tq:] + tri, top, tq)
    s_ref[0] = scores(jnp.maximum(top - 2, 0))
    update(jnp.concatenate([s_ref[1, :, :tq] + tri, s_ref[1, :, tq:]], axis=1), top - 1)

    def live():
        return jnp.max(carry_ref[...]) > NEGLIGIBLE

    def cond(state):
        t, alive = state
        return jnp.logical_and(t < i, alive)

    def body(state):
        t, _ = state
        j = top - 2 - 2 * t
        s_ref[1] = scores(j - 1)
        update(s_ref[0], j)
        s_ref[0] = scores(jnp.maximum(j - 2, 0))
        update(s_ref[1], j - 1)
        return t + 1, live()

    lax.while_loop(cond, body, (0, live()))
    even_head, odd_head = _split_side_by_side(acc_ref[...], tk)
    head_row = lax.broadcasted_iota(jnp.int32, (LANES, tq), 0) < HEAD_DIM
    o_t = jnp.where(head_row, even_head, odd_head)
    o_ref[...] = o_t.T.astype(BF16)


def _stick_attention(qkv, *, batch, seq):
    tq, tk = B_TQ, B_TK
    assert tq == 2 * tk
    nq = seq // tq
    n_tiles = B_HEADS * HEAD_DIM // LANES
    kern = functools.partial(_stick_kernel, tq=tq, tk=tk)
    return pl.pallas_call(
        kern,
        out_shape=jax.ShapeDtypeStruct((batch * seq, B_HEADS * HEAD_DIM), BF16),
        grid=(batch, n_tiles, nq),
        in_specs=[pl.BlockSpec((tq, LANES), lambda b, g, i: (b * nq + i, g)),
                  pl.BlockSpec((seq, LANES), lambda b, g, i: (b, n_tiles + g)),
                  pl.BlockSpec((seq, LANES), lambda b, g, i: (b, 2 * n_tiles + g))],
        out_specs=pl.BlockSpec((tq, LANES), lambda b, g, i: (b * nq + i, g)),
        scratch_shapes=[pltpu.VMEM((seq, LANES), BF16),
                        pltpu.VMEM((seq // tk, LANES, tk), BF16),
                        pltpu.VMEM((tk, tq), F32),
                        pltpu.VMEM((2, tk, 2 * tq), F32),
                        pltpu.VMEM((1, 2 * tq), F32),
                        pltpu.VMEM((LANES, 2 * tq), F32)],
        compiler_params=_params(("arbitrary", "arbitrary", "arbitrary")),
        name="stick_breaking_attention",
    )(qkv, qkv, qkv)


def _swa_kernel(sink_ref, q_ref, kprev_ref, kcur_ref, vprev_ref, vcur_ref, o_ref, *, tq):
    i = pl.program_id(1)
    w = WINDOW
    qi = lax.broadcasted_iota(jnp.int32, (w, 2 * w), 0)
    ki = lax.broadcasted_iota(jnp.int32, (w, 2 * w), 1)
    rel = w + qi - ki
    band = (rel >= 0) & (rel < w)
    lane = lax.broadcasted_iota(jnp.int32, (w, LANES), 1)
    low = lane < HEAD_DIM
    n_qtiles = C_HEADS * HEAD_DIM // LANES
    tiles_per_group = n_qtiles // C_KV_HEADS
    for sub in range(tq // w):
        rows = slice(sub * w, (sub + 1) * w)
        if sub == 0:
            k_prev, v_prev = kprev_ref[...], vprev_ref[...]
            mask = band & ((ki >= w) | (i > 0))
        else:
            prev = slice((sub - 1) * w, sub * w)
            k_prev, v_prev = kcur_ref[prev, :], vcur_ref[prev, :]
            mask = band
        kwin = jnp.concatenate([k_prev, kcur_ref[rows, :]], axis=0)
        vwin = jnp.concatenate([v_prev, vcur_ref[rows, :]], axis=0)
        for t in range(n_qtiles):
            grp = t // tiles_per_group
            kg = kwin[:, grp * LANES:(grp + 1) * LANES]
            vg = vwin[:, grp * LANES:(grp + 1) * LANES]
            q = q_ref[rows, t * LANES:(t + 1) * LANES]
            zero = jnp.zeros_like(q)
            outs = []
            for c in range(2):
                qc = jnp.where(low, q, zero) if c == 0 else jnp.where(low, zero, q)
                sc = jnp.where(mask, _dot_nt(qc, kg), NEG_BIG)
                sink = sink_ref[2 * t + c] * LOG2E
                mx = jnp.maximum(jnp.max(sc, axis=-1, keepdims=True), sink)
                p = jnp.exp2(sc - mx)
                denom = jnp.sum(p, axis=-1, keepdims=True) + jnp.exp2(sink - mx)
                outs.append(_dot(p.astype(BF16), vg) / denom)
            o_ref[rows, t * LANES:(t + 1) * LANES] = jnp.where(low, outs[0], outs[1]).astype(BF16)


def _swa_attention(qkv, sinks, *, batch, seq):
    tq = C_TQ
    w = WINDOW
    nq = seq // tq
    per = tq // w
    nq_cols = C_HEADS * HEAD_DIM
    kcol = nq_cols // (2 * LANES)
    vcol = kcol + 1

    def prev_map(col):
        return lambda b, i: (jnp.maximum((b * nq + i) * per - 1, 0), col)

    def cur_map(col):
        return lambda b, i: (b * nq + i, col)

    return pl.pallas_call(
        functools.partial(_swa_kernel, tq=tq),
        out_shape=jax.ShapeDtypeStruct((batch * seq, nq_cols), BF16),
        grid=(batch, nq),
        in_specs=[pl.BlockSpec(memory_space=pltpu.SMEM),
                  pl.BlockSpec((tq, nq_cols), lambda b, i: (b * nq + i, 0)),
                  pl.BlockSpec((w, 2 * LANES), prev_map(kcol)),
                  pl.BlockSpec((tq, 2 * LANES), cur_map(kcol)),
                  pl.BlockSpec((w, 2 * LANES), prev_map(vcol)),
                  pl.BlockSpec((tq, 2 * LANES), cur_map(vcol))],
        out_specs=pl.BlockSpec((tq, nq_cols), lambda b, i: (b * nq + i, 0)),
        compiler_params=_params(("arbitrary", "arbitrary")),
        name="sliding_window_attention",
    )(sinks, qkv, qkv, qkv, qkv, qkv)


def kernel(x, positions, norm_gains, a_w_in, a_w_out, a_lambda, a_subln, b_w_in, b_w_out,
           c_w_in, c_w_out, c_sinks, ffn_w_gate, ffn_w_up, ffn_w_down):
    batch, seq, d = x.shape
    depth = norm_gains.shape[0]
    tables = _rope_tables(positions)
    h = x.reshape(batch * seq, d)
    for i in range(depth):
        kind = i % N_MIXERS
        inst = i // N_MIXERS
        g = norm_gains[i]
        if kind == 0:
            qkv = _in_proj(h, g[0], a_w_in[inst].astype(BF16), tables, n_rope=16, n_q=8,
                           q_scale=ATTN_SCALE * LOG2E)
            mix = _diff_attention(qkv, a_lambda[inst], a_subln[inst], batch=batch, seq=seq, layer_idx=i)
            w_out = a_w_out[inst]
        elif kind == 1:
            qkv = _in_proj(h, g[0], b_w_in[inst].astype(BF16), None, n_rope=0, n_q=8,
                           q_scale=ATTN_SCALE * 0.5)
            mix = _stick_attention(qkv, batch=batch, seq=seq)
            w_out = b_w_out[inst]
        else:
            w = c_w_in[inst]
            nq_cols = C_HEADS * HEAD_DIM
            kv = w[:, nq_cols:].reshape(d, 2 * C_KV_HEADS, 1, HEAD_DIM)
            kv = jnp.broadcast_to(kv, (d, 2 * C_KV_HEADS, 2, HEAD_DIM)).reshape(d, 4 * C_KV_HEADS * HEAD_DIM)
            w = jnp.concatenate([w[:, :nq_cols], kv], axis=1).astype(BF16)
            qkv = _in_proj(h, g[0], w, tables, n_rope=10, n_q=8, q_scale=ATTN_SCALE * LOG2E)
            mix = _swa_attention(qkv, c_sinks[inst], batch=batch, seq=seq)
            w_out = c_w_out[inst]
        h = _post_mixer(mix, w_out.astype(BF16), g[1], h, g[2], ffn_w_gate[i].astype(BF16),
                        ffn_w_up[i].astype(BF16), ffn_w_down[i].astype(BF16), g[3])
    return h.reshape(batch, seq, d)
```

```python
import functools
import math

import jax
import jax.numpy as jnp
from jax import lax
from jax.experimental import pallas as pl
from jax.experimental.pallas import tpu as pltpu

F32 = jnp.float32
BF16 = jnp.bfloat16

D_MODEL = 1024
N_MIXERS = 3
ROPE_THETA = 10000.0
ROPE_DIM = 64
NORM_EPS = 1e-6
HEAD_DIM = 64
ATTN_SCALE = HEAD_DIM ** -0.5
A_HEADS = 8
B_HEADS = 16
C_HEADS = 16
C_KV_HEADS = 2
WINDOW = 128

LANES = 128
NEG_BIG = -1e30
NEGLIGIBLE = 2.0 ** -100
VMEM_LIMIT = 56 * 1024 * 1024

ROW_TILE = 1024
A_TQ = 512
A_TK = 256
B_TQ = 512
B_TK = 256
C_TQ = 512
LOG2E = math.log2(math.e)
SUM_ROWS = 16

def _params(sem):
    return pltpu.CompilerParams(dimension_semantics=sem, vmem_limit_bytes=VMEM_LIMIT)


def _rms(x, g):
    ms = jnp.mean(x * x, axis=-1, keepdims=True)
    return x * lax.rsqrt(ms + NORM_EPS) * g


def _dot(a, b):
    return jnp.dot(a, b, preferred_element_type=F32)


def _dot_nt(a, b):
    return lax.dot_general(a, b, (((1,), (1,)), ((), ())), preferred_element_type=F32)


def _halves_side_by_side(q, blk):
    lane = lax.broadcasted_iota(jnp.int32, q.shape, 1)
    zero = jnp.zeros_like(q)
    first = jnp.where(lane < HEAD_DIM, q, zero)
    second = jnp.where(lane >= HEAD_DIM, q, zero)
    parts = []
    for g in range(q.shape[0] // blk):
        rows = slice(g * blk, (g + 1) * blk)
        parts += [first[rows], second[rows]]
    return jnp.concatenate(parts, axis=0)


def _split_side_by_side(x, blk):
    groups = x.shape[1] // (2 * blk)
    first = [x[:, 2 * g * blk:(2 * g + 1) * blk] for g in range(groups)]
    second = [x[:, (2 * g + 1) * blk:(2 * g + 2) * blk] for g in range(groups)]
    return jnp.concatenate(first, axis=1), jnp.concatenate(second, axis=1)


def _rope_table_kernel(pos_ref, inv_ref, sign_ref, cos_ref, sin_ref):
    ang = pos_ref[...].astype(F32) * inv_ref[...]
    cos_ref[...] = jnp.cos(ang)
    sin_ref[...] = jnp.sin(ang) * sign_ref[...]


def _rope_tables(positions):
    m = positions.size
    inv = ROPE_THETA ** (-jnp.arange(0, ROPE_DIM, 2, dtype=F32) / ROPE_DIM)
    inv128 = jnp.tile(inv, LANES // (ROPE_DIM // 2)).reshape(1, LANES)
    half = ROPE_DIM // 2
    sign = jnp.where((jnp.arange(LANES) % ROPE_DIM) < half, -1.0, 1.0).astype(F32).reshape(1, LANES)
    tm = 2048
    return pl.pallas_call(
        _rope_table_kernel,
        out_shape=(jax.ShapeDtypeStruct((m, LANES), F32), jax.ShapeDtypeStruct((m, LANES), F32)),
        grid=(m // tm,),
        in_specs=[pl.BlockSpec((tm, 1), lambda i: (i, 0)),
                  pl.BlockSpec((1, LANES), lambda i: (0, 0)),
                  pl.BlockSpec((1, LANES), lambda i: (0, 0))],
        out_specs=(pl.BlockSpec((tm, LANES), lambda i: (i, 0)),
                   pl.BlockSpec((tm, LANES), lambda i: (i, 0))),
        compiler_params=_params(("arbitrary",)),
        name="rope_tables",
    )(positions.reshape(m, 1), inv128, sign)


def _swap_halves(x):
    lane = lax.broadcasted_iota(jnp.int32, x.shape, 1)
    fwd = pltpu.roll(x, ROPE_DIM // 2, 1)
    bwd = pltpu.roll(x, LANES - ROPE_DIM // 2, 1)
    return jnp.where((lane & (ROPE_DIM - 1)) < ROPE_DIM // 2, bwd, fwd)


def _in_proj_kernel(*refs, n_tiles, n_rope, n_q, q_scale, chunk):
    if n_rope:
        x_ref, g_ref, w_ref, cos_ref, sin_ref, o_ref = refs
    else:
        x_ref, g_ref, w_ref, o_ref = refs
    hn = _rms(x_ref[...], g_ref[...]).astype(BF16)
    if n_rope:
        cos = cos_ref[...]
        sin = sin_ref[...]
    tiles_per_chunk = chunk // LANES
    for c in range(n_tiles // tiles_per_chunk):
        acc = _dot(hn, w_ref[:, c * chunk:(c + 1) * chunk])
        for t in range(tiles_per_chunk):
            tile = c * tiles_per_chunk + t
            a = acc[:, t * LANES:(t + 1) * LANES]
            if tile < n_rope:
                a = a * cos + _swap_halves(a) * sin
            if tile < n_q:
                a = a * q_scale
            o_ref[:, tile * LANES:(tile + 1) * LANES] = a.astype(BF16)


def _in_proj(x, g, w, tables, *, n_rope, n_q, q_scale):
    m = x.shape[0]
    n = w.shape[1]
    tm = ROW_TILE
    chunk = 256
    kern = functools.partial(_in_proj_kernel, n_tiles=n // LANES, n_rope=n_rope, n_q=n_q,
                             q_scale=q_scale, chunk=chunk)
    in_specs = [pl.BlockSpec((tm, D_MODEL), lambda i: (i, 0)),
                pl.BlockSpec((1, D_MODEL), lambda i: (0, 0)),
                pl.BlockSpec((D_MODEL, n), lambda i: (0, 0))]
    args = [x, g.reshape(1, D_MODEL), w]
    if n_rope:
        in_specs += [pl.BlockSpec((tm, LANES), lambda i: (i, 0)),
                     pl.BlockSpec((tm, LANES), lambda i: (i, 0))]
        args += list(tables)
    return pl.pallas_call(
        kern,
        out_shape=jax.ShapeDtypeStruct((m, n), BF16),
        grid=(m // tm,),
        in_specs=in_specs,
        out_specs=pl.BlockSpec((tm, n), lambda i: (i, 0)),
        compiler_params=_params(("arbitrary",)),
        name="in_proj",
    )(*args)


def _post_mixer_kernel(mix_ref, wo_ref, gmix_ref, x_ref, gin_ref, wg_ref, wu_ref, wd_ref, gout_ref,
                       o_ref, act_ref, *, chunk):
    x = x_ref[...] + _rms(_dot(mix_ref[...], wo_ref[...]), gmix_ref[...])
    hn = _rms(x, gin_ref[...]).astype(BF16)
    hidden = wg_ref.shape[1]
    for c in range(hidden // chunk):
        cols = slice(c * chunk, (c + 1) * chunk)
        gate = _dot(hn, wg_ref[:, cols])
        up = _dot(hn, wu_ref[:, cols])
        act_ref[:, cols] = (gate * jax.nn.sigmoid(gate) * up).astype(BF16)
    f = _dot(act_ref[...], wd_ref[...])
    o_ref[...] = x + _rms(f, gout_ref[...])


def _post_mixer(mix, wo, gmix, x, gin, wg, wu, wd, gout):
    m = x.shape[0]
    hidden = wg.shape[1]
    tm = ROW_TILE
    const = lambda i: (0, 0)
    rows = lambda i: (i, 0)
    resident = functools.partial(pl.BlockSpec, index_map=const, pipeline_mode=pl.Buffered(1))
    gain = pl.BlockSpec((1, D_MODEL), const)
    return pl.pallas_call(
        functools.partial(_post_mixer_kernel, chunk=256),
        out_shape=jax.ShapeDtypeStruct((m, D_MODEL), F32),
        grid=(m // tm,),
        in_specs=[pl.BlockSpec((tm, D_MODEL), rows),
                  resident((D_MODEL, D_MODEL)),
                  gain,
                  pl.BlockSpec((tm, D_MODEL), rows),
                  gain,
                  resident((D_MODEL, hidden)),
                  resident((D_MODEL, hidden)),
                  resident((hidden, D_MODEL)),
                  gain],
        out_specs=pl.BlockSpec((tm, D_MODEL), rows),
        scratch_shapes=[pltpu.VMEM((tm, hidden), BF16)],
        compiler_params=_params(("arbitrary",)),
        name="out_proj_ffn",
    )(mix, wo, gmix.reshape(1, D_MODEL), x, gin.reshape(1, D_MODEL), wg, wu, wd,
      gout.reshape(1, D_MODEL))


def _diff_attn_kernel(lam_ref, subln_ref, q_ref, k_ref, v_ref, o_ref,
                      vt_ref, tri_ref, s_ref, m_ref, acc_ref, *, tq, tk, lam_init):
    i = pl.program_id(2)
    n = 2 * tq

    @pl.when(i == 0)
    def _():
        ones_row = jnp.where(lax.broadcasted_iota(jnp.int32, (SUM_ROWS, tk), 0) == 0, 1.0, 0.0)
        for c in range(vt_ref.shape[0]):
            vt_ref[c, 0:LANES, :] = v_ref[c * tk:(c + 1) * tk, :].astype(F32).T.astype(BF16)
            vt_ref[c, LANES:, :] = ones_row.astype(BF16)
        key = lax.broadcasted_iota(jnp.int32, (tk, tq), 0)
        qry = lax.broadcasted_iota(jnp.int32, (tk, tq), 1) & (tk - 1)
        tri_ref[...] = jnp.where(key <= qry, 0.0, NEG_BIG)

    def query_columns(step):
        q = q_ref[pl.ds(pl.multiple_of(step * tq, tq), tq), :]
        return _halves_side_by_side(q, tk)

    qcat = query_columns(i)

    m_ref[...] = jnp.full(m_ref.shape, NEG_BIG, F32)
    acc_ref[...] = jnp.zeros(acc_ref.shape, F32)

    def scores(j, lo=0):
        kb = k_ref[pl.ds(pl.multiple_of(j * tk, tk), tk), :]
        return _dot_nt(kb, qcat[lo:])

    @pl.when(i == 0)
    def _():
        s_ref[0] = scores(0)

    def update(s, j, lo=0):
        m_old = m_ref[:, lo:]
        m_new = jnp.maximum(m_old, jnp.max(s, axis=0, keepdims=True))
        alpha = jnp.exp2(m_old - m_new)
        p = jnp.exp2(s - m_new).astype(BF16)
        acc_ref[:, lo:] = alpha * acc_ref[:, lo:] + _dot(vt_ref[j], p)
        m_ref[:, lo:] = m_new

    def pair(j):
        s_ref[1] = scores(j + 1)
        update(s_ref[0], j)
        s_ref[0] = scores(j + 2)
        update(s_ref[1], j + 1)

    def body(t, carry):
        pair(4 * t)
        pair(4 * t + 2)
        return carry

    lax.fori_loop(0, i >> 1, body, 0)

    @pl.when((i & 1) == 1)
    def _():
        pair(2 * i - 2)

    j = 2 * i
    s_ref[1, :, tq:] = scores(j + 1, tq)
    tri = tri_ref[...]
    update(jnp.concatenate([s_ref[0, :, :tq] + tri, s_ref[0, :, tq:]], axis=1), j)
    update(s_ref[1, :, tq:] + tri, j + 1, tq)

    nxt = jnp.minimum(i + 1, pl.num_programs(2) - 1)
    s_ref[0] = _dot_nt(k_ref[0:tk, :], query_columns(nxt))

    lp = lam_ref[...]
    lam = (jnp.exp(jnp.sum(lp[0:1] * lp[1:2], axis=-1, keepdims=True))
           - jnp.exp(jnp.sum(lp[2:3] * lp[3:4], axis=-1, keepdims=True)) + lam_init)
    acc = acc_ref[...]
    first, second = _split_side_by_side(acc[:LANES] / acc[LANES:LANES + 1], tk)
    o_t = first - lam * second
    ms = jnp.mean(o_t * o_t, axis=0, keepdims=True)
    o_t = o_t * lax.rsqrt(ms + NORM_EPS) * (subln_ref[...] * (1.0 - lam_init))
    o_ref[...] = o_t.T.astype(BF16)


def _diff_attention(qkv, lam_params, subln_g, *, batch, seq, layer_idx):
    tq, tk = A_TQ, A_TK
    assert tq == 2 * tk
    nq = seq // tq
    lam_init = 0.8 - 0.6 * math.exp(-0.3 * layer_idx)
    kern = functools.partial(_diff_attn_kernel, tq=tq, tk=tk, lam_init=lam_init)
    return pl.pallas_call(
        kern,
        out_shape=jax.ShapeDtypeStruct((batch * seq, A_HEADS * LANES), BF16),
        grid=(batch, A_HEADS, nq),
        in_specs=[pl.BlockSpec((4, HEAD_DIM), lambda b, h, i: (0, 0)),
                  pl.BlockSpec((LANES, 1), lambda b, h, i: (0, 0)),
                  pl.BlockSpec((seq, LANES), lambda b, h, i: (b, h)),
                  pl.BlockSpec((seq, LANES), lambda b, h, i: (b, A_HEADS + h)),
                  pl.BlockSpec((seq, LANES), lambda b, h, i: (b, 2 * A_HEADS + h))],
        out_specs=pl.BlockSpec((tq, LANES), lambda b, h, i: (b * nq + i, h)),
        scratch_shapes=[pltpu.VMEM((seq // tk, LANES + SUM_ROWS, tk), BF16),
                        pltpu.VMEM((tk, tq), F32),
                        pltpu.VMEM((2, tk, 2 * tq), F32),
                        pltpu.VMEM((1, 2 * tq), F32),
                        pltpu.VMEM((LANES + SUM_ROWS, 2 * tq), F32)],
        compiler_params=_params(("arbitrary", "arbitrary", "arbitrary")),
        name="diff_attention",
    )(lam_params, subln_g.reshape(LANES, 1), qkv, qkv, qkv)


def _stick_kernel(q_ref, k_ref, v_ref, o_ref, kp_ref, vt_ref, tri_ref, s_ref, carry_ref, acc_ref,
                  *, tq, tk):
    i = pl.program_id(2)
    n = 2 * tq
    nv = tk // 8

    @pl.when(i == 0)
    def _():
        r = lax.broadcasted_iota(jnp.int32, (tk, tk), 0)
        c = lax.broadcasted_iota(jnp.int32, (tk, tk), 1)
        perm = jnp.where(c == (r & 7) * nv + (r >> 3), 1.0, 0.0).astype(BF16)

        for blk in range(vt_ref.shape[0]):
            rows = slice(blk * tk, (blk + 1) * tk)
            kv = _dot(perm, jnp.concatenate([k_ref[rows, :], v_ref[rows, :]], axis=1))
            kp_ref[rows, :] = kv[:, :LANES].astype(BF16)
            vt_ref[blk] = kv[:, LANES:].T.astype(BF16)
        row = lax.broadcasted_iota(jnp.int32, (tk, tq), 0)
        kpos = (row & 7) * nv + (row >> 3)
        qry = lax.broadcasted_iota(jnp.int32, (tk, tq), 1) & (tk - 1)
        tri_ref[...] = jnp.where(kpos < qry, 0.0, NEG_BIG)

    qcat = _halves_side_by_side(q_ref[...], tk)

    carry_ref[...] = jnp.ones(carry_ref.shape, F32)
    acc_ref[...] = jnp.zeros(acc_ref.shape, F32)

    def scores(j, lo=0):
        kb = kp_ref[pl.ds(pl.multiple_of(j * tk, tk), tk), :]
        return _dot_nt(kb, qcat[lo:])


    def update(zh, j, lo=0):
        w = n - lo
        beta = 0.5 + 0.5 * jnp.tanh(zh)
        run = jnp.ones((8, w), F32)
        bw = [None] * nv
        for v in reversed(range(nv)):
            rows = slice(v * 8, (v + 1) * 8)
            bw[v] = beta[rows] * run
            run = run - bw[v]
        sub = lax.broadcasted_iota(jnp.int32, (8, w), 0)
        later = jnp.ones((8, w), F32)
        for sp in range(7, 0, -1):
            later = jnp.where(sub < sp, later * run[sp:sp + 1, :], later)
        carry = carry_ref[:, lo:]
        scale = later * carry
        a = jnp.concatenate([bw[v] * scale for v in range(nv)], axis=0).astype(BF16)
        acc_ref[:, lo:] += _dot(vt_ref[j], a)
        carry_ref[:, lo:] = carry * (later[0:1, :] * run[0:1, :])

    top = 2 * i + 1

    def block(j):
        return jnp.maximum(j, 0)

    def absent(j):
        return jnp.where(j < 0, NEG_BIG, 0.0)

    s_ref[0, :, tq:] = scores(top, tq)
    s_ref[1] = scores(top - 1)
    tri = tri_ref[...]
    update(s_ref[0, :, tq:] + tri, top, tq)
    s_ref[0] = scores(block(top - 2))
    update(jnp.concatenate([s_ref[1, :, :tq] + tri, s_ref[1, :, tq:]], axis=1), top - 1)
    s_ref[1] = scores(block(top - 3))
    update(s_ref[0] + absent(top - 2), block(top - 2))

    def live():
        return jnp.max(carry_ref[...]) > NEGLIGIBLE

    def cond(state):
        t, alive = state
        return jnp.logical_and(t < i, alive)

    def body(state):
        t, _ = state
        j = top - 3 - 2 * t
        s_ref[0] = scores(block(j - 1))
        update(s_ref[1], j)
        s_ref[1] = scores(block(j - 2))
        update(s_ref[0] + absent(j - 1), block(j - 1))
        return t + 1, live()

    lax.while_loop(cond, body, (0, live()))
    even_head, odd_head = _split_side_by_side(acc_ref[...], tk)
    head_row = lax.broadcasted_iota(jnp.int32, (LANES, tq), 0) < HEAD_DIM
    o_t = jnp.where(head_row, even_head, odd_head)
    o_ref[...] = o_t.T.astype(BF16)


def _stick_attention(qkv, *, batch, seq):
    tq, tk = B_TQ, B_TK
    assert tq == 2 * tk
    nq = seq // tq
    n_tiles = B_HEADS * HEAD_DIM // LANES
    kern = functools.partial(_stick_kernel, tq=tq, tk=tk)
    return pl.pallas_call(
        kern,
        out_shape=jax.ShapeDtypeStruct((batch * seq, B_HEADS * HEAD_DIM), BF16),
        grid=(batch, n_tiles, nq),
        in_specs=[pl.BlockSpec((tq, LANES), lambda b, g, i: (b * nq + i, g)),
                  pl.BlockSpec((seq, LANES), lambda b, g, i: (b, n_tiles + g)),
                  pl.BlockSpec((seq, LANES), lambda b, g, i: (b, 2 * n_tiles + g))],
        out_specs=pl.BlockSpec((tq, LANES), lambda b, g, i: (b * nq + i, g)),
        scratch_shapes=[pltpu.VMEM((seq, LANES), BF16),
                        pltpu.VMEM((seq // tk, LANES, tk), BF16),
                        pltpu.VMEM((tk, tq), F32),
                        pltpu.VMEM((2, tk, 2 * tq), F32),
                        pltpu.VMEM((1, 2 * tq), F32),
                        pltpu.VMEM((LANES, 2 * tq), F32)],
        compiler_params=_params(("arbitrary", "arbitrary", "arbitrary")),
        name="stick_breaking_attention",
    )(qkv, qkv, qkv)


def _swa_kernel(sink_ref, q_ref, kprev_ref, kcur_ref, vprev_ref, vcur_ref, o_ref, *, tq):
    i = pl.program_id(1)
    w = WINDOW
    qi = lax.broadcasted_iota(jnp.int32, (w, 2 * w), 0)
    ki = lax.broadcasted_iota(jnp.int32, (w, 2 * w), 1)
    rel = w + qi - ki
    band = (rel >= 0) & (rel < w)
    lane = lax.broadcasted_iota(jnp.int32, (w, LANES), 1)
    low = lane < HEAD_DIM
    n_qtiles = C_HEADS * HEAD_DIM // LANES
    tiles_per_group = n_qtiles // C_KV_HEADS
    for sub in range(tq // w):
        rows = slice(sub * w, (sub + 1) * w)
        if sub == 0:
            k_prev, v_prev = kprev_ref[...], vprev_ref[...]
            mask = band & ((ki >= w) | (i > 0))
        else:
            prev = slice((sub - 1) * w, sub * w)
            k_prev, v_prev = kcur_ref[prev, :], vcur_ref[prev, :]
            mask = band
        kwin = jnp.concatenate([k_prev, kcur_ref[rows, :]], axis=0)
        vwin = jnp.concatenate([v_prev, vcur_ref[rows, :]], axis=0)
        for t in range(n_qtiles):
            grp = t // tiles_per_group
            kg = kwin[:, grp * LANES:(grp + 1) * LANES]
            vg = vwin[:, grp * LANES:(grp + 1) * LANES]
            q = q_ref[rows, t * LANES:(t + 1) * LANES]
            zero = jnp.zeros_like(q)
            outs = []
            for c in range(2):
                qc = jnp.where(low, q, zero) if c == 0 else jnp.where(low, zero, q)
                sc = jnp.where(mask, _dot_nt(qc, kg), NEG_BIG)
                sink = sink_ref[2 * t + c] * LOG2E
                mx = jnp.maximum(jnp.max(sc, axis=-1, keepdims=True), sink)
                p = jnp.exp2(sc - mx)
                denom = jnp.sum(p, axis=-1, keepdims=True) + jnp.exp2(sink - mx)
                outs.append(_dot(p.astype(BF16), vg) / denom)
            o_ref[rows, t * LANES:(t + 1) * LANES] = jnp.where(low, outs[0], outs[1]).astype(BF16)


def _swa_attention(qkv, sinks, *, batch, seq):
    tq = C_TQ
    w = WINDOW
    nq = seq // tq
    per = tq // w
    nq_cols = C_HEADS * HEAD_DIM
    kcol = nq_cols // (2 * LANES)
    vcol = kcol + 1

    def prev_map(col):
        return lambda b, i: (jnp.maximum((b * nq + i) * per - 1, 0), col)

    def cur_map(col):
        return lambda b, i: (b * nq + i, col)

    return pl.pallas_call(
        functools.partial(_swa_kernel, tq=tq),
        out_shape=jax.ShapeDtypeStruct((batch * seq, nq_cols), BF16),
        grid=(batch, nq),
        in_specs=[pl.BlockSpec(memory_space=pltpu.SMEM),
                  pl.BlockSpec((tq, nq_cols), lambda b, i: (b * nq + i, 0)),
                  pl.BlockSpec((w, 2 * LANES), prev_map(kcol)),
                  pl.BlockSpec((tq, 2 * LANES), cur_map(kcol)),
                  pl.BlockSpec((w, 2 * LANES), prev_map(vcol)),
                  pl.BlockSpec((tq, 2 * LANES), cur_map(vcol))],
        out_specs=pl.BlockSpec((tq, nq_cols), lambda b, i: (b * nq + i, 0)),
        compiler_params=_params(("arbitrary", "arbitrary")),
        name="sliding_window_attention",
    )(sinks, qkv, qkv, qkv, qkv, qkv)


def kernel(x, positions, norm_gains, a_w_in, a_w_out, a_lambda, a_subln, b_w_in, b_w_out,
           c_w_in, c_w_out, c_sinks, ffn_w_gate, ffn_w_up, ffn_w_down):
    batch, seq, d = x.shape
    depth = norm_gains.shape[0]
    tables = _rope_tables(positions)
    h = x.reshape(batch * seq, d)
    for i in range(depth):
        kind = i % N_MIXERS
        inst = i // N_MIXERS
        g = norm_gains[i]
        if kind == 0:
            qkv = _in_proj(h, g[0], a_w_in[inst].astype(BF16), tables, n_rope=16, n_q=8,
                           q_scale=ATTN_SCALE * LOG2E)
            mix = _diff_attention(qkv, a_lambda[inst], a_subln[inst], batch=batch, seq=seq, layer_idx=i)
            w_out = a_w_out[inst]
        elif kind == 1:
            qkv = _in_proj(h, g[0], b_w_in[inst].astype(BF16), None, n_rope=0, n_q=8,
                           q_scale=ATTN_SCALE * 0.5)
            mix = _stick_attention(qkv, batch=batch, seq=seq)
            w_out = b_w_out[inst]
        else:
            w = c_w_in[inst]
            nq_cols = C_HEADS * HEAD_DIM
            kv = w[:, nq_cols:].reshape(d, 2 * C_KV_HEADS, 1, HEAD_DIM)
            kv = jnp.broadcast_to(kv, (d, 2 * C_KV_HEADS, 2, HEAD_DIM)).reshape(d, 4 * C_KV_HEADS * HEAD_DIM)
            w = jnp.concatenate([w[:, :nq_cols], kv], axis=1).astype(BF16)
            qkv = _in_proj(h, g[0], w, tables, n_rope=10, n_q=8, q_scale=ATTN_SCALE * LOG2E)
            mix = _swa_attention(qkv, c_sinks[inst], batch=batch, seq=seq)
            w_out = c_w_out[inst]
        h = _post_mixer(mix, w_out.astype(BF16), g[1], h, g[2], ffn_w_gate[i].astype(BF16),
                        ffn_w_up[i].astype(BF16), ffn_w_down[i].astype(BF16), g[3])
    return h.reshape(batch, seq, d)
```

```python
import functools
import math

import jax
import jax.numpy as jnp
from jax import lax
from jax.experimental import pallas as pl
from jax.experimental.pallas import tpu as pltpu

F32 = jnp.float32
BF16 = jnp.bfloat16

D_MODEL = 1024
N_MIXERS = 3
ROPE_THETA = 10000.0
ROPE_DIM = 64
NORM_EPS = 1e-6
HEAD_DIM = 64
ATTN_SCALE = HEAD_DIM ** -0.5
A_HEADS = 8
B_HEADS = 16
C_HEADS = 16
C_KV_HEADS = 2
WINDOW = 128

LANES = 128
NEG_BIG = -1e30
NEGLIGIBLE = 2.0 ** -100
VMEM_LIMIT = 56 * 1024 * 1024

ROW_TILE = 1024
A_TQ = 1024
A_TK = 256
B_TQ = 512
B_TK = 256
C_TQ = 512
LOG2E = math.log2(math.e)
SUM_ROWS = 16

def _params(sem):
    return pltpu.CompilerParams(dimension_semantics=sem, vmem_limit_bytes=VMEM_LIMIT)


def _rms(x, g):
    ms = jnp.mean(x * x, axis=-1, keepdims=True)
    return x * lax.rsqrt(ms + NORM_EPS) * g


def _dot(a, b):
    return jnp.dot(a, b, preferred_element_type=F32)


def _dot_nt(a, b):
    return lax.dot_general(a, b, (((1,), (1,)), ((), ())), preferred_element_type=F32)


def _halves_side_by_side(q, blk):
    lane = lax.broadcasted_iota(jnp.int32, q.shape, 1)
    zero = jnp.zeros_like(q)
    first = jnp.where(lane < HEAD_DIM, q, zero)
    second = jnp.where(lane >= HEAD_DIM, q, zero)
    parts = []
    for g in range(q.shape[0] // blk):
        rows = slice(g * blk, (g + 1) * blk)
        parts += [first[rows], second[rows]]
    return jnp.concatenate(parts, axis=0)


def _split_side_by_side(x, blk):
    groups = x.shape[1] // (2 * blk)
    first = [x[:, 2 * g * blk:(2 * g + 1) * blk] for g in range(groups)]
    second = [x[:, (2 * g + 1) * blk:(2 * g + 2) * blk] for g in range(groups)]
    return jnp.concatenate(first, axis=1), jnp.concatenate(second, axis=1)


def _rope_table_kernel(pos_ref, inv_ref, sign_ref, cos_ref, sin_ref):
    ang = pos_ref[...].astype(F32) * inv_ref[...]
    cos_ref[...] = jnp.cos(ang)
    sin_ref[...] = jnp.sin(ang) * sign_ref[...]


def _rope_tables(positions):
    m = positions.size
    inv = ROPE_THETA ** (-jnp.arange(0, ROPE_DIM, 2, dtype=F32) / ROPE_DIM)
    inv128 = jnp.tile(inv, LANES // (ROPE_DIM // 2)).reshape(1, LANES)
    half = ROPE_DIM // 2
    sign = jnp.where((jnp.arange(LANES) % ROPE_DIM) < half, -1.0, 1.0).astype(F32).reshape(1, LANES)
    tm = 2048
    return pl.pallas_call(
        _rope_table_kernel,
        out_shape=(jax.ShapeDtypeStruct((m, LANES), F32), jax.ShapeDtypeStruct((m, LANES), F32)),
        grid=(m // tm,),
        in_specs=[pl.BlockSpec((tm, 1), lambda i: (i, 0)),
                  pl.BlockSpec((1, LANES), lambda i: (0, 0)),
                  pl.BlockSpec((1, LANES), lambda i: (0, 0))],
        out_specs=(pl.BlockSpec((tm, LANES), lambda i: (i, 0)),
                   pl.BlockSpec((tm, LANES), lambda i: (i, 0))),
        compiler_params=_params(("arbitrary",)),
        name="rope_tables",
    )(positions.reshape(m, 1), inv128, sign)


def _swap_halves(x):
    lane = lax.broadcasted_iota(jnp.int32, x.shape, 1)
    fwd = pltpu.roll(x, ROPE_DIM // 2, 1)
    bwd = pltpu.roll(x, LANES - ROPE_DIM // 2, 1)
    return jnp.where((lane & (ROPE_DIM - 1)) < ROPE_DIM // 2, bwd, fwd)


def _in_proj_kernel(*refs, n_tiles, n_rope, n_q, q_scale, chunk):
    if n_rope:
        x_ref, g_ref, w_ref, cos_ref, sin_ref, o_ref = refs
    else:
        x_ref, g_ref, w_ref, o_ref = refs
    half = x_ref.shape[0] // 2
    parts = [slice(r * half, (r + 1) * half) for r in range(2)]
    hns = [_rms(x_ref[rows, :], g_ref[...]).astype(BF16) for rows in parts]
    tiles_per_chunk = chunk // LANES
    for rows, hn in zip(parts, hns):
        if n_rope:
            cos = cos_ref[rows, :]
            sin = sin_ref[rows, :]
        for c in range(n_tiles // tiles_per_chunk):
            acc = _dot(hn, w_ref[:, c * chunk:(c + 1) * chunk])
            for t in range(tiles_per_chunk):
                tile = c * tiles_per_chunk + t
                a = acc[:, t * LANES:(t + 1) * LANES]
                if tile < n_rope:
                    a = a * cos + _swap_halves(a) * sin
                if tile < n_q:
                    a = a * q_scale
                o_ref[rows, tile * LANES:(tile + 1) * LANES] = a.astype(BF16)


def _in_proj(x, g, w, tables, *, n_rope, n_q, q_scale):
    m = x.shape[0]
    n = w.shape[1]
    tm = ROW_TILE
    chunk = 256
    kern = functools.partial(_in_proj_kernel, n_tiles=n // LANES, n_rope=n_rope, n_q=n_q,
                             q_scale=q_scale, chunk=chunk)
    in_specs = [pl.BlockSpec((tm, D_MODEL), lambda i: (i, 0)),
                pl.BlockSpec((1, D_MODEL), lambda i: (0, 0)),
                pl.BlockSpec((D_MODEL, n), lambda i: (0, 0))]
    args = [x, g.reshape(1, D_MODEL), w]
    if n_rope:
        in_specs += [pl.BlockSpec((tm, LANES), lambda i: (i, 0)),
                     pl.BlockSpec((tm, LANES), lambda i: (i, 0))]
        args += list(tables)
    return pl.pallas_call(
        kern,
        out_shape=jax.ShapeDtypeStruct((m, n), BF16),
        grid=(m // tm,),
        in_specs=in_specs,
        out_specs=pl.BlockSpec((tm, n), lambda i: (i, 0)),
        compiler_params=_params(("arbitrary",)),
        name="in_proj",
    )(*args)


def _post_mixer_kernel(mix_ref, wo_ref, gmix_ref, x_ref, gin_ref, wg_ref, wu_ref, wd_ref, gout_ref,
                       o_ref, act_ref, *, chunk):
    hidden = wg_ref.shape[1]
    half = x_ref.shape[0] // 2
    parts = [slice(r * half, (r + 1) * half) for r in range(2)]
    xs, hns = [], []
    for rows in parts:
        x = x_ref[rows, :] + _rms(_dot(mix_ref[rows, :], wo_ref[...]), gmix_ref[...])
        xs.append(x)
        hns.append(_rms(x, gin_ref[...]).astype(BF16))
    for rows, hn in zip(parts, hns):
        for c in range(hidden // chunk):
            cols = slice(c * chunk, (c + 1) * chunk)
            gate = _dot(hn, wg_ref[:, cols])
            up = _dot(hn, wu_ref[:, cols])
            act_ref[rows, cols] = (gate * jax.nn.sigmoid(gate) * up).astype(BF16)
    for rows, x in zip(parts, xs):
        f = _dot(act_ref[rows, :], wd_ref[...])
        o_ref[rows, :] = x + _rms(f, gout_ref[...])


def _post_mixer(mix, wo, gmix, x, gin, wg, wu, wd, gout):
    m = x.shape[0]
    hidden = wg.shape[1]
    tm = ROW_TILE
    const = lambda i: (0, 0)
    rows = lambda i: (i, 0)
    resident = functools.partial(pl.BlockSpec, index_map=const, pipeline_mode=pl.Buffered(1))
    gain = pl.BlockSpec((1, D_MODEL), const)
    return pl.pallas_call(
        functools.partial(_post_mixer_kernel, chunk=256),
        out_shape=jax.ShapeDtypeStruct((m, D_MODEL), F32),
        grid=(m // tm,),
        in_specs=[pl.BlockSpec((tm, D_MODEL), rows),
                  resident((D_MODEL, D_MODEL)),
                  gain,
                  pl.BlockSpec((tm, D_MODEL), rows),
                  gain,
                  resident((D_MODEL, hidden)),
                  resident((D_MODEL, hidden)),
                  resident((hidden, D_MODEL)),
                  gain],
        out_specs=pl.BlockSpec((tm, D_MODEL), rows),
        scratch_shapes=[pltpu.VMEM((tm, hidden), BF16)],
        compiler_params=_params(("arbitrary",)),
        name="out_proj_ffn",
    )(mix, wo, gmix.reshape(1, D_MODEL), x, gin.reshape(1, D_MODEL), wg, wu, wd,
      gout.reshape(1, D_MODEL))


def _diff_attn_kernel(lam_ref, subln_ref, q_ref, k_ref, v_ref, o_ref,
                      vt_ref, tri_ref, s_ref, m_ref, acc_ref, *, tq, tk, lam_init):
    i = pl.program_id(2)
    n = 2 * tq

    @pl.when(i == 0)
    def _():
        ones_row = jnp.where(lax.broadcasted_iota(jnp.int32, (SUM_ROWS, tk), 0) == 0, 1.0, 0.0)
        for c in range(vt_ref.shape[0]):
            vt_ref[c, 0:LANES, :] = v_ref[c * tk:(c + 1) * tk, :].astype(F32).T.astype(BF16)
            vt_ref[c, LANES:, :] = ones_row.astype(BF16)
        key = lax.broadcasted_iota(jnp.int32, tri_ref.shape, 0)
        qry = lax.broadcasted_iota(jnp.int32, tri_ref.shape, 1) & (tk - 1)
        tri_ref[...] = jnp.where(key <= qry, 0.0, NEG_BIG)

    def query_columns(step):
        q = q_ref[pl.ds(pl.multiple_of(step * tq, tq), tq), :]
        return _halves_side_by_side(q, tk)

    qcat = query_columns(i)

    m_ref[...] = jnp.full(m_ref.shape, NEG_BIG, F32)
    acc_ref[...] = jnp.zeros(acc_ref.shape, F32)

    def scores(j, lo=0):
        kb = k_ref[pl.ds(pl.multiple_of(j * tk, tk), tk), :]
        return _dot_nt(kb, qcat[lo:])

    @pl.when(i == 0)
    def _():
        s_ref[0] = scores(0)

    def update(s, j, lo=0):
        m_old = m_ref[:, lo:]
        m_new = jnp.maximum(m_old, jnp.max(s, axis=0, keepdims=True))
        alpha = jnp.exp2(m_old - m_new)
        p = jnp.exp2(s - m_new).astype(BF16)
        acc_ref[:, lo:] = alpha * acc_ref[:, lo:] + _dot(vt_ref[j], p)
        m_ref[:, lo:] = m_new

    def pair(j):
        s_ref[1] = scores(j + 1)
        update(s_ref[0], j)
        s_ref[0] = scores(j + 2)
        update(s_ref[1], j + 1)

    def body(t, carry):
        pair(4 * t)
        pair(4 * t + 2)
        return carry

    ratio = tq // tk
    below = ratio * i
    lax.fori_loop(0, below >> 2, body, 0)

    if ratio % 4:
        @pl.when((below & 3) == 2)
        def _():
            pair(below - 2)

    tri = tri_ref[...]
    group = 2 * tk
    for d in range(ratio):
        lo = d * group
        if d + 1 < ratio:
            s_ref[(d + 1) % 2, :, lo + group:] = scores(below + d + 1, lo + group)
        s = s_ref[d % 2, :, lo:]
        if d + 1 < ratio:
            s = jnp.concatenate([s[:, :group] + tri, s[:, group:]], axis=1)
        else:
            s = s + tri
        update(s, below + d, lo)

    nxt = jnp.minimum(i + 1, pl.num_programs(2) - 1)
    s_ref[0] = _dot_nt(k_ref[0:tk, :], query_columns(nxt))

    lp = lam_ref[...]
    lam = (jnp.exp(jnp.sum(lp[0:1] * lp[1:2], axis=-1, keepdims=True))
           - jnp.exp(jnp.sum(lp[2:3] * lp[3:4], axis=-1, keepdims=True)) + lam_init)
    acc = acc_ref[...]
    first, second = _split_side_by_side(acc[:LANES] / acc[LANES:LANES + 1], tk)
    o_t = first - lam * second
    ms = jnp.mean(o_t * o_t, axis=0, keepdims=True)
    o_t = o_t * lax.rsqrt(ms + NORM_EPS) * (subln_ref[...] * (1.0 - lam_init))
    o_ref[...] = o_t.T.astype(BF16)


def _diff_attention(qkv, lam_params, subln_g, *, batch, seq, layer_idx):
    tq, tk = A_TQ, A_TK
    assert tq % (2 * tk) == 0
    nq = seq // tq
    lam_init = 0.8 - 0.6 * math.exp(-0.3 * layer_idx)
    kern = functools.partial(_diff_attn_kernel, tq=tq, tk=tk, lam_init=lam_init)
    return pl.pallas_call(
        kern,
        out_shape=jax.ShapeDtypeStruct((batch * seq, A_HEADS * LANES), BF16),
        grid=(batch, A_HEADS, nq),
        in_specs=[pl.BlockSpec((4, HEAD_DIM), lambda b, h, i: (0, 0)),
                  pl.BlockSpec((LANES, 1), lambda b, h, i: (0, 0)),
                  pl.BlockSpec((seq, LANES), lambda b, h, i: (b, h)),
                  pl.BlockSpec((seq, LANES), lambda b, h, i: (b, A_HEADS + h)),
                  pl.BlockSpec((seq, LANES), lambda b, h, i: (b, 2 * A_HEADS + h))],
        out_specs=pl.BlockSpec((tq, LANES), lambda b, h, i: (b * nq + i, h)),
        scratch_shapes=[pltpu.VMEM((seq // tk, LANES + SUM_ROWS, tk), BF16),
                        pltpu.VMEM((tk, 2 * tk), F32),
                        pltpu.VMEM((2, tk, 2 * tq), F32),
                        pltpu.VMEM((1, 2 * tq), F32),
                        pltpu.VMEM((LANES + SUM_ROWS, 2 * tq), F32)],
        compiler_params=_params(("arbitrary", "arbitrary", "arbitrary")),
        name="diff_attention",
    )(lam_params, subln_g.reshape(LANES, 1), qkv, qkv, qkv)


def _stick_kernel(q_ref, k_ref, v_ref, o_ref, kp_ref, vt_ref, tri_ref, s_ref, carry_ref, acc_ref,
                  *, tq, tk):
    i = pl.program_id(2)
    n = 2 * tq
    nv = tk // 8

    @pl.when(i == 0)
    def _():
        r = lax.broadcasted_iota(jnp.int32, (tk, tk), 0)
        c = lax.broadcasted_iota(jnp.int32, (tk, tk), 1)
        perm = jnp.where(c == (r & 7) * nv + (r >> 3), 1.0, 0.0).astype(BF16)

        for blk in range(vt_ref.shape[0]):
            rows = slice(blk * tk, (blk + 1) * tk)
            kv = _dot(perm, jnp.concatenate([k_ref[rows, :], v_ref[rows, :]], axis=1))
            kp_ref[rows, :] = kv[:, :LANES].astype(BF16)
            vt_ref[blk] = kv[:, LANES:].T.astype(BF16)
        row = lax.broadcasted_iota(jnp.int32, (tk, tq), 0)
        kpos = (row & 7) * nv + (row >> 3)
        qry = lax.broadcasted_iota(jnp.int32, (tk, tq), 1) & (tk - 1)
        tri_ref[...] = jnp.where(kpos < qry, 0.0, NEG_BIG)

    qcat = _halves_side_by_side(q_ref[...], tk)

    carry_ref[...] = jnp.ones(carry_ref.shape, F32)
    acc_ref[...] = jnp.zeros(acc_ref.shape, F32)

    def scores(j, lo=0):
        kb = kp_ref[pl.ds(pl.multiple_of(j * tk, tk), tk), :]
        return _dot_nt(kb, qcat[lo:])


    def update(zh, j, lo=0):
        w = n - lo
        beta = 0.5 + 0.5 * jnp.tanh(zh)
        run = jnp.ones((8, w), F32)
        bw = [None] * nv
        for v in reversed(range(nv)):
            rows = slice(v * 8, (v + 1) * 8)
            bw[v] = beta[rows] * run
            run = run - bw[v]
        sub = lax.broadcasted_iota(jnp.int32, (8, w), 0)
        later = jnp.ones((8, w), F32)
        for sp in range(7, 0, -1):
            later = jnp.where(sub < sp, later * run[sp:sp + 1, :], later)
        carry = carry_ref[:, lo:]
        scale = later * carry
        a = jnp.concatenate([bw[v] * scale for v in range(nv)], axis=0).astype(BF16)
        acc_ref[:, lo:] += _dot(vt_ref[j], a)
        carry_ref[:, lo:] = carry * (later[0:1, :] * run[0:1, :])

    top = 2 * i + 1

    def block(j):
        return jnp.maximum(j, 0)

    def absent(j):
        return jnp.where(j < 0, NEG_BIG, 0.0)

    s_ref[0, :, tq:] = scores(top, tq)
    s_ref[1] = scores(top - 1)
    tri = tri_ref[...]
    update(s_ref[0, :, tq:] + tri, top, tq)
    s_ref[0] = scores(block(top - 2))
    update(jnp.concatenate([s_ref[1, :, :tq] + tri, s_ref[1, :, tq:]], axis=1), top - 1)
    s_ref[1] = scores(block(top - 3))
    update(s_ref[0] + absent(top - 2), block(top - 2))

    def live():
        return jnp.max(carry_ref[...]) > NEGLIGIBLE

    def cond(state):
        t, alive = state
        return jnp.logical_and(t < i, alive)

    def body(state):
        t, _ = state
        j = top - 3 - 2 * t
        s_ref[0] = scores(block(j - 1))
        update(s_ref[1], j)
        s_ref[1] = scores(block(j - 2))
        update(s_ref[0] + absent(j - 1), block(j - 1))
        return t + 1, live()

    lax.while_loop(cond, body, (0, live()))
    even_head, odd_head = _split_side_by_side(acc_ref[...], tk)
    head_row = lax.broadcasted_iota(jnp.int32, (LANES, tq), 0) < HEAD_DIM
    o_t = jnp.where(head_row, even_head, odd_head)
    o_ref[...] = o_t.T.astype(BF16)


def _stick_attention(qkv, *, batch, seq):
    tq, tk = B_TQ, B_TK
    assert tq == 2 * tk
    nq = seq // tq
    n_tiles = B_HEADS * HEAD_DIM // LANES
    kern = functools.partial(_stick_kernel, tq=tq, tk=tk)
    return pl.pallas_call(
        kern,
        out_shape=jax.ShapeDtypeStruct((batch * seq, B_HEADS * HEAD_DIM), BF16),
        grid=(batch, n_tiles, nq),
        in_specs=[pl.BlockSpec((tq, LANES), lambda b, g, i: (b * nq + i, g)),
                  pl.BlockSpec((seq, LANES), lambda b, g, i: (b, n_tiles + g)),
                  pl.BlockSpec((seq, LANES), lambda b, g, i: (b, 2 * n_tiles + g))],
        out_specs=pl.BlockSpec((tq, LANES), lambda b, g, i: (b * nq + i, g)),
        scratch_shapes=[pltpu.VMEM((seq, LANES), BF16),
                        pltpu.VMEM((seq // tk, LANES, tk), BF16),
                        pltpu.VMEM((tk, tq), F32),
                        pltpu.VMEM((2, tk, 2 * tq), F32),
                        pltpu.VMEM((1, 2 * tq), F32),
                        pltpu.VMEM((LANES, 2 * tq), F32)],
        compiler_params=_params(("arbitrary", "arbitrary", "arbitrary")),
        name="stick_breaking_attention",
    )(qkv, qkv, qkv)


def _swa_kernel(sink_ref, q_ref, kprev_ref, kcur_ref, vprev_ref, vcur_ref, o_ref, *, tq):
    i = pl.program_id(1)
    w = WINDOW
    qi = lax.broadcasted_iota(jnp.int32, (w, 2 * w), 0)
    ki = lax.broadcasted_iota(jnp.int32, (w, 2 * w), 1)
    rel = w + qi - ki
    band = (rel >= 0) & (rel < w)
    lane = lax.broadcasted_iota(jnp.int32, (w, LANES), 1)
    low = lane < HEAD_DIM
    n_qtiles = C_HEADS * HEAD_DIM // LANES
    tiles_per_group = n_qtiles // C_KV_HEADS
    for sub in range(tq // w):
        rows = slice(sub * w, (sub + 1) * w)
        if sub == 0:
            k_prev, v_prev = kprev_ref[...], vprev_ref[...]
            mask = band & ((ki >= w) | (i > 0))
        else:
            prev = slice((sub - 1) * w, sub * w)
            k_prev, v_prev = kcur_ref[prev, :], vcur_ref[prev, :]
            mask = band
        kwin = jnp.concatenate([k_prev, kcur_ref[rows, :]], axis=0)
        vwin = jnp.concatenate([v_prev, vcur_ref[rows, :]], axis=0)
        for t in range(n_qtiles):
            grp = t // tiles_per_group
            kg = kwin[:, grp * LANES:(grp + 1) * LANES]
            vg = vwin[:, grp * LANES:(grp + 1) * LANES]
            q = q_ref[rows, t * LANES:(t + 1) * LANES]
            zero = jnp.zeros_like(q)
            outs = []
            for c in range(2):
                qc = jnp.where(low, q, zero) if c == 0 else jnp.where(low, zero, q)
                sc = jnp.where(mask, _dot_nt(qc, kg), NEG_BIG)
                sink = sink_ref[2 * t + c] * LOG2E
                mx = jnp.maximum(jnp.max(sc, axis=-1, keepdims=True), sink)
                p = jnp.exp2(sc - mx)
                denom = jnp.sum(p, axis=-1, keepdims=True) + jnp.exp2(sink - mx)
                outs.append(_dot(p.astype(BF16), vg) / denom)
            o_ref[rows, t * LANES:(t + 1) * LANES] = jnp.where(low, outs[0], outs[1]).astype(BF16)


def _swa_attention(qkv, sinks, *, batch, seq):
    tq = C_TQ
    w = WINDOW
    nq = seq // tq
    per = tq // w
    nq_cols = C_HEADS * HEAD_DIM
    kcol = nq_cols // (2 * LANES)
    vcol = kcol + 1

    def prev_map(col):
        return lambda b, i: (jnp.maximum((b * nq + i) * per - 1, 0), col)

    def cur_map(col):
        return lambda b, i: (b * nq + i, col)

    return pl.pallas_call(
        functools.partial(_swa_kernel, tq=tq),
        out_shape=jax.ShapeDtypeStruct((batch * seq, nq_cols), BF16),
        grid=(batch, nq),
        in_specs=[pl.BlockSpec(memory_space=pltpu.SMEM),
                  pl.BlockSpec((tq, nq_cols), lambda b, i: (b * nq + i, 0)),
                  pl.BlockSpec((w, 2 * LANES), prev_map(kcol)),
                  pl.BlockSpec((tq, 2 * LANES), cur_map(kcol)),
                  pl.BlockSpec((w, 2 * LANES), prev_map(vcol)),
                  pl.BlockSpec((tq, 2 * LANES), cur_map(vcol))],
        out_specs=pl.BlockSpec((tq, nq_cols), lambda b, i: (b * nq + i, 0)),
        compiler_params=_params(("arbitrary", "arbitrary")),
        name="sliding_window_attention",
    )(sinks, qkv, qkv, qkv, qkv, qkv)


def kernel(x, positions, norm_gains, a_w_in, a_w_out, a_lambda, a_subln, b_w_in, b_w_out,
           c_w_in, c_w_out, c_sinks, ffn_w_gate, ffn_w_up, ffn_w_down):
    batch, seq, d = x.shape
    depth = norm_gains.shape[0]
    tables = _rope_tables(positions)
    h = x.reshape(batch * seq, d)
    for i in range(depth):
        kind = i % N_MIXERS
        inst = i // N_MIXERS
        g = norm_gains[i]
        if kind == 0:
            qkv = _in_proj(h, g[0], a_w_in[inst].astype(BF16), tables, n_rope=16, n_q=8,
                           q_scale=ATTN_SCALE * LOG2E)
            mix = _diff_attention(qkv, a_lambda[inst], a_subln[inst], batch=batch, seq=seq, layer_idx=i)
            w_out = a_w_out[inst]
        elif kind == 1:
            qkv = _in_proj(h, g[0], b_w_in[inst].astype(BF16), None, n_rope=0, n_q=8,
                           q_scale=ATTN_SCALE * 0.5)
            mix = _stick_attention(qkv, batch=batch, seq=seq)
            w_out = b_w_out[inst]
        else:
            w = c_w_in[inst]
            nq_cols = C_HEADS * HEAD_DIM
            kv = w[:, nq_cols:].reshape(d, 2 * C_KV_HEADS, 1, HEAD_DIM)
            kv = jnp.broadcast_to(kv, (d, 2 * C_KV_HEADS, 2, HEAD_DIM)).reshape(d, 4 * C_KV_HEADS * HEAD_DIM)
            w = jnp.concatenate([w[:, :nq_cols], kv], axis=1).astype(BF16)
            qkv = _in_proj(h, g[0], w, tables, n_rope=10, n_q=8, q_scale=ATTN_SCALE * LOG2E)
            mix = _swa_attention(qkv, c_sinks[inst], batch=batch, seq=seq)
            w_out = c_w_out[inst]
        h = _post_mixer(mix, w_out.astype(BF16), g[1], h, g[2], ffn_w_gate[i].astype(BF16),
                        ffn_w_up[i].astype(BF16), ffn_w_down[i].astype(BF16), g[3])
    return h.reshape(batch, seq, d)
```

```python
import functools
import math

import jax
import jax.numpy as jnp
from jax import lax
from jax.experimental import pallas as pl
from jax.experimental.pallas import tpu as pltpu

F32 = jnp.float32
BF16 = jnp.bfloat16

D_MODEL = 1024
N_MIXERS = 3
ROPE_THETA = 10000.0
ROPE_DIM = 64
NORM_EPS = 1e-6
HEAD_DIM = 64
ATTN_SCALE = HEAD_DIM ** -0.5
A_HEADS = 8
B_HEADS = 16
C_HEADS = 16
C_KV_HEADS = 2
WINDOW = 128

LANES = 128
NEG_BIG = -1e30
NEGLIGIBLE = 2.0 ** -100
VMEM_LIMIT = 56 * 1024 * 1024

ROW_TILE = 1024
FFN_PARTS = 4
A_TQ = 1024
A_TK = 256
B_TQ = 512
B_TK = 256
C_TQ = 512
LOG2E = math.log2(math.e)
SUM_ROWS = 16

def _params(sem):
    return pltpu.CompilerParams(dimension_semantics=sem, vmem_limit_bytes=VMEM_LIMIT)


def _rms(x, g):
    ms = jnp.mean(x * x, axis=-1, keepdims=True)
    return x * lax.rsqrt(ms + NORM_EPS) * g


def _dot(a, b):
    return jnp.dot(a, b, preferred_element_type=F32)


def _dot_nt(a, b):
    return lax.dot_general(a, b, (((1,), (1,)), ((), ())), preferred_element_type=F32)


def _halves_side_by_side(q, blk):
    lane = lax.broadcasted_iota(jnp.int32, q.shape, 1)
    zero = jnp.zeros_like(q)
    first = jnp.where(lane < HEAD_DIM, q, zero)
    second = jnp.where(lane >= HEAD_DIM, q, zero)
    parts = []
    for g in range(q.shape[0] // blk):
        rows = slice(g * blk, (g + 1) * blk)
        parts += [first[rows], second[rows]]
    return jnp.concatenate(parts, axis=0)


def _split_side_by_side(x, blk):
    groups = x.shape[1] // (2 * blk)
    first = [x[:, 2 * g * blk:(2 * g + 1) * blk] for g in range(groups)]
    second = [x[:, (2 * g + 1) * blk:(2 * g + 2) * blk] for g in range(groups)]
    return jnp.concatenate(first, axis=1), jnp.concatenate(second, axis=1)


def _rope_table_kernel(pos_ref, inv_ref, sign_ref, cos_ref, sin_ref):
    ang = pos_ref[...].astype(F32) * inv_ref[...]
    cos_ref[...] = jnp.cos(ang)
    sin_ref[...] = jnp.sin(ang) * sign_ref[...]


def _rope_tables(positions):
    m = positions.size
    inv = ROPE_THETA ** (-jnp.arange(0, ROPE_DIM, 2, dtype=F32) / ROPE_DIM)
    inv128 = jnp.tile(inv, LANES // (ROPE_DIM // 2)).reshape(1, LANES)
    half = ROPE_DIM // 2
    sign = jnp.where((jnp.arange(LANES) % ROPE_DIM) < half, -1.0, 1.0).astype(F32).reshape(1, LANES)
    tm = 2048
    return pl.pallas_call(
        _rope_table_kernel,
        out_shape=(jax.ShapeDtypeStruct((m, LANES), F32), jax.ShapeDtypeStruct((m, LANES), F32)),
        grid=(m // tm,),
        in_specs=[pl.BlockSpec((tm, 1), lambda i: (i, 0)),
                  pl.BlockSpec((1, LANES), lambda i: (0, 0)),
                  pl.BlockSpec((1, LANES), lambda i: (0, 0))],
        out_specs=(pl.BlockSpec((tm, LANES), lambda i: (i, 0)),
                   pl.BlockSpec((tm, LANES), lambda i: (i, 0))),
        compiler_params=_params(("arbitrary",)),
        name="rope_tables",
    )(positions.reshape(m, 1), inv128, sign)


def _swap_halves(x):
    lane = lax.broadcasted_iota(jnp.int32, x.shape, 1)
    fwd = pltpu.roll(x, ROPE_DIM // 2, 1)
    bwd = pltpu.roll(x, LANES - ROPE_DIM // 2, 1)
    return jnp.where((lane & (ROPE_DIM - 1)) < ROPE_DIM // 2, bwd, fwd)


def _in_proj_kernel(*refs, n_tiles, n_rope, n_q, q_scale, chunk):
    if n_rope:
        x_ref, g_ref, w_ref, cos_ref, sin_ref, o_ref = refs
    else:
        x_ref, g_ref, w_ref, o_ref = refs
    half = x_ref.shape[0] // 2
    parts = [slice(r * half, (r + 1) * half) for r in range(2)]
    hns = [_rms(x_ref[rows, :], g_ref[...]).astype(BF16) for rows in parts]
    tiles_per_chunk = chunk // LANES
    for rows, hn in zip(parts, hns):
        if n_rope:
            cos = cos_ref[rows, :]
            sin = sin_ref[rows, :]
        for c in range(n_tiles // tiles_per_chunk):
            acc = _dot(hn, w_ref[:, c * chunk:(c + 1) * chunk])
            for t in range(tiles_per_chunk):
                tile = c * tiles_per_chunk + t
                a = acc[:, t * LANES:(t + 1) * LANES]
                if tile < n_rope:
                    a = a * cos + _swap_halves(a) * sin
                if tile < n_q:
                    a = a * q_scale
                o_ref[rows, tile * LANES:(tile + 1) * LANES] = a.astype(BF16)


def _in_proj(x, g, w, tables, *, n_rope, n_q, q_scale):
    m = x.shape[0]
    n = w.shape[1]
    tm = ROW_TILE
    chunk = 256
    kern = functools.partial(_in_proj_kernel, n_tiles=n // LANES, n_rope=n_rope, n_q=n_q,
                             q_scale=q_scale, chunk=chunk)
    in_specs = [pl.BlockSpec((tm, D_MODEL), lambda i: (i, 0)),
                pl.BlockSpec((1, D_MODEL), lambda i: (0, 0)),
                pl.BlockSpec((D_MODEL, n), lambda i: (0, 0))]
    args = [x, g.reshape(1, D_MODEL), w]
    if n_rope:
        in_specs += [pl.BlockSpec((tm, LANES), lambda i: (i, 0)),
                     pl.BlockSpec((tm, LANES), lambda i: (i, 0))]
        args += list(tables)
    return pl.pallas_call(
        kern,
        out_shape=jax.ShapeDtypeStruct((m, n), BF16),
        grid=(m // tm,),
        in_specs=in_specs,
        out_specs=pl.BlockSpec((tm, n), lambda i: (i, 0)),
        compiler_params=_params(("arbitrary",)),
        name="in_proj",
    )(*args)


def _post_mixer_kernel(mix_ref, wo_ref, gmix_ref, x_ref, gin_ref, wg_ref, wu_ref, wd_ref, gout_ref,
                       o_ref, act_ref, *, chunk):
    hidden = wg_ref.shape[1]
    part = x_ref.shape[0] // FFN_PARTS
    parts = [slice(r * part, (r + 1) * part) for r in range(FFN_PARTS)]
    xs, hns = [], []
    for rows in parts:
        x = x_ref[rows, :] + _rms(_dot(mix_ref[rows, :], wo_ref[...]), gmix_ref[...])
        xs.append(x)
        hns.append(_rms(x, gin_ref[...]).astype(BF16))
    for rows, hn in zip(parts, hns):
        for c in range(hidden // chunk):
            cols = slice(c * chunk, (c + 1) * chunk)
            gate = _dot(hn, wg_ref[:, cols])
            up = _dot(hn, wu_ref[:, cols])
            act_ref[rows, cols] = (gate * jax.nn.sigmoid(gate) * up).astype(BF16)
    for rows, x in zip(parts, xs):
        f = _dot(act_ref[rows, :], wd_ref[...])
        o_ref[rows, :] = x + _rms(f, gout_ref[...])


def _post_mixer(mix, wo, gmix, x, gin, wg, wu, wd, gout):
    m = x.shape[0]
    hidden = wg.shape[1]
    tm = ROW_TILE
    const = lambda i: (0, 0)
    rows = lambda i: (i, 0)
    resident = functools.partial(pl.BlockSpec, index_map=const, pipeline_mode=pl.Buffered(1))
    gain = pl.BlockSpec((1, D_MODEL), const)
    return pl.pallas_call(
        functools.partial(_post_mixer_kernel, chunk=256),
        out_shape=jax.ShapeDtypeStruct((m, D_MODEL), F32),
        grid=(m // tm,),
        in_specs=[pl.BlockSpec((tm, D_MODEL), rows),
                  resident((D_MODEL, D_MODEL)),
                  gain,
                  pl.BlockSpec((tm, D_MODEL), rows),
                  gain,
                  resident((D_MODEL, hidden)),
                  resident((D_MODEL, hidden)),
                  resident((hidden, D_MODEL)),
                  gain],
        out_specs=pl.BlockSpec((tm, D_MODEL), rows),
        scratch_shapes=[pltpu.VMEM((tm, hidden), BF16)],
        compiler_params=_params(("arbitrary",)),
        name="out_proj_ffn",
    )(mix, wo, gmix.reshape(1, D_MODEL), x, gin.reshape(1, D_MODEL), wg, wu, wd,
      gout.reshape(1, D_MODEL))


def _diff_attn_kernel(lam_ref, subln_ref, q_ref, k_ref, v_ref, o_ref,
                      vt_ref, tri_ref, s_ref, m_ref, acc_ref, *, tq, tk, lam_init):
    i = pl.program_id(2)
    n = 2 * tq

    @pl.when(i == 0)
    def _():
        ones_row = jnp.where(lax.broadcasted_iota(jnp.int32, (SUM_ROWS, tk), 0) == 0, 1.0, 0.0)
        for c in range(vt_ref.shape[0]):
            vt_ref[c, 0:LANES, :] = v_ref[c * tk:(c + 1) * tk, :].astype(F32).T.astype(BF16)
            vt_ref[c, LANES:, :] = ones_row.astype(BF16)
        key = lax.broadcasted_iota(jnp.int32, tri_ref.shape, 0)
        qry = lax.broadcasted_iota(jnp.int32, tri_ref.shape, 1) & (tk - 1)
        tri_ref[...] = jnp.where(key <= qry, 0.0, NEG_BIG)

    def query_columns(step):
        q = q_ref[pl.ds(pl.multiple_of(step * tq, tq), tq), :]
        return _halves_side_by_side(q, tk)

    qcat = query_columns(i)

    m_ref[...] = jnp.full(m_ref.shape, NEG_BIG, F32)
    acc_ref[...] = jnp.zeros(acc_ref.shape, F32)

    def scores(j, lo=0):
        kb = k_ref[pl.ds(pl.multiple_of(j * tk, tk), tk), :]
        return _dot_nt(kb, qcat[lo:])

    @pl.when(i == 0)
    def _():
        s_ref[0] = scores(0)

    def update(s, j, lo=0):
        m_old = m_ref[:, lo:]
        m_new = jnp.maximum(m_old, jnp.max(s, axis=0, keepdims=True))
        alpha = jnp.exp2(m_old - m_new)
        p = jnp.exp2(s - m_new).astype(BF16)
        acc_ref[:, lo:] = alpha * acc_ref[:, lo:] + _dot(vt_ref[j], p)
        m_ref[:, lo:] = m_new

    def pair(j):
        s_ref[1] = scores(j + 1)
        update(s_ref[0], j)
        s_ref[0] = scores(j + 2)
        update(s_ref[1], j + 1)

    def body(t, carry):
        pair(4 * t)
        pair(4 * t + 2)
        return carry

    ratio = tq // tk
    below = ratio * i
    lax.fori_loop(0, below >> 2, body, 0)

    if ratio % 4:
        @pl.when((below & 3) == 2)
        def _():
            pair(below - 2)

    tri = tri_ref[...]
    group = 2 * tk
    for d in range(ratio):
        lo = d * group
        if d + 1 < ratio:
            s_ref[(d + 1) % 2, :, lo + group:] = scores(below + d + 1, lo + group)
        s = s_ref[d % 2, :, lo:]
        if d + 1 < ratio:
            s = jnp.concatenate([s[:, :group] + tri, s[:, group:]], axis=1)
        else:
            s = s + tri
        update(s, below + d, lo)

    nxt = jnp.minimum(i + 1, pl.num_programs(2) - 1)
    s_ref[0] = _dot_nt(k_ref[0:tk, :], query_columns(nxt))

    lp = lam_ref[...]
    lam = (jnp.exp(jnp.sum(lp[0:1] * lp[1:2], axis=-1, keepdims=True))
           - jnp.exp(jnp.sum(lp[2:3] * lp[3:4], axis=-1, keepdims=True)) + lam_init)
    acc = acc_ref[...]
    first, second = _split_side_by_side(acc[:LANES] / acc[LANES:LANES + 1], tk)
    o_t = first - lam * second
    ms = jnp.mean(o_t * o_t, axis=0, keepdims=True)
    o_t = o_t * lax.rsqrt(ms + NORM_EPS) * (subln_ref[...] * (1.0 - lam_init))
    o_ref[...] = o_t.T.astype(BF16)


def _diff_attention(qkv, lam_params, subln_g, *, batch, seq, layer_idx):
    tq, tk = A_TQ, A_TK
    assert tq % (2 * tk) == 0
    nq = seq // tq
    lam_init = 0.8 - 0.6 * math.exp(-0.3 * layer_idx)
    kern = functools.partial(_diff_attn_kernel, tq=tq, tk=tk, lam_init=lam_init)
    return pl.pallas_call(
        kern,
        out_shape=jax.ShapeDtypeStruct((batch * seq, A_HEADS * LANES), BF16),
        grid=(batch, A_HEADS, nq),
        in_specs=[pl.BlockSpec((4, HEAD_DIM), lambda b, h, i: (0, 0)),
                  pl.BlockSpec((LANES, 1), lambda b, h, i: (0, 0)),
                  pl.BlockSpec((seq, LANES), lambda b, h, i: (b, h)),
                  pl.BlockSpec((seq, LANES), lambda b, h, i: (b, A_HEADS + h)),
                  pl.BlockSpec((seq, LANES), lambda b, h, i: (b, 2 * A_HEADS + h))],
        out_specs=pl.BlockSpec((tq, LANES), lambda b, h, i: (b * nq + i, h)),
        scratch_shapes=[pltpu.VMEM((seq // tk, LANES + SUM_ROWS, tk), BF16),
                        pltpu.VMEM((tk, 2 * tk), F32),
                        pltpu.VMEM((2, tk, 2 * tq), F32),
                        pltpu.VMEM((1, 2 * tq), F32),
                        pltpu.VMEM((LANES + SUM_ROWS, 2 * tq), F32)],
        compiler_params=_params(("arbitrary", "arbitrary", "arbitrary")),
        name="diff_attention",
    )(lam_params, subln_g.reshape(LANES, 1), qkv, qkv, qkv)


def _stick_kernel(q_ref, k_ref, v_ref, o_ref, kp_ref, vt_ref, tri_ref, s_ref, carry_ref, acc_ref,
                  *, tq, tk):
    i = pl.program_id(2)
    n = 2 * tq
    nv = tk // 8

    @pl.when(i == 0)
    def _():
        r = lax.broadcasted_iota(jnp.int32, (tk, tk), 0)
        c = lax.broadcasted_iota(jnp.int32, (tk, tk), 1)
        perm = jnp.where(c == (r & 7) * nv + (r >> 3), 1.0, 0.0).astype(BF16)

        for blk in range(vt_ref.shape[0]):
            rows = slice(blk * tk, (blk + 1) * tk)
            kv = _dot(perm, jnp.concatenate([k_ref[rows, :], v_ref[rows, :]], axis=1))
            kp_ref[rows, :] = kv[:, :LANES].astype(BF16)
            vt_ref[blk] = kv[:, LANES:].T.astype(BF16)
        row = lax.broadcasted_iota(jnp.int32, (tk, tq), 0)
        kpos = (row & 7) * nv + (row >> 3)
        qry = lax.broadcasted_iota(jnp.int32, (tk, tq), 1) & (tk - 1)
        tri_ref[...] = jnp.where(kpos < qry, 0.0, NEG_BIG)

    qcat = _halves_side_by_side(q_ref[...], tk)

    carry_ref[...] = jnp.ones(carry_ref.shape, F32)
    acc_ref[...] = jnp.zeros(acc_ref.shape, F32)

    def scores(j, lo=0):
        kb = kp_ref[pl.ds(pl.multiple_of(j * tk, tk), tk), :]
        return _dot_nt(kb, qcat[lo:])


    def update(zh, j, lo=0):
        w = n - lo
        beta = 0.5 + 0.5 * jnp.tanh(zh)
        run = jnp.ones((8, w), F32)
        bw = [None] * nv
        for v in reversed(range(nv)):
            rows = slice(v * 8, (v + 1) * 8)
            bw[v] = beta[rows] * run
            run = run - bw[v]
        sub = lax.broadcasted_iota(jnp.int32, (8, w), 0)
        later = jnp.ones((8, w), F32)
        for sp in range(7, 0, -1):
            later = jnp.where(sub < sp, later * run[sp:sp + 1, :], later)
        carry = carry_ref[:, lo:]
        scale = later * carry
        a = jnp.concatenate([bw[v] * scale for v in range(nv)], axis=0).astype(BF16)
        acc_ref[:, lo:] += _dot(vt_ref[j], a)
        carry_ref[:, lo:] = carry * (later[0:1, :] * run[0:1, :])

    top = 2 * i + 1

    def block(j):
        return jnp.maximum(j, 0)

    def absent(j):
        return jnp.where(j < 0, NEG_BIG, 0.0)

    s_ref[0, :, tq:] = scores(top, tq)
    s_ref[1] = scores(top - 1)
    tri = tri_ref[...]
    update(s_ref[0, :, tq:] + tri, top, tq)
    s_ref[0] = scores(block(top - 2))
    update(jnp.concatenate([s_ref[1, :, :tq] + tri, s_ref[1, :, tq:]], axis=1), top - 1)
    s_ref[1] = scores(block(top - 3))
    update(s_ref[0] + absent(top - 2), block(top - 2))

    def live():
        return jnp.max(carry_ref[...]) > NEGLIGIBLE

    def cond(state):
        t, alive = state
        return jnp.logical_and(t < i, alive)

    def body(state):
        t, _ = state
        j = top - 3 - 2 * t
        s_ref[0] = scores(block(j - 1))
        update(s_ref[1], j)
        s_ref[1] = scores(block(j - 2))
        update(s_ref[0] + absent(j - 1), block(j - 1))
        return t + 1, live()

    lax.while_loop(cond, body, (0, live()))
    even_head, odd_head = _split_side_by_side(acc_ref[...], tk)
    head_row = lax.broadcasted_iota(jnp.int32, (LANES, tq), 0) < HEAD_DIM
    o_t = jnp.where(head_row, even_head, odd_head)
    o_ref[...] = o_t.T.astype(BF16)


def _stick_attention(qkv, *, batch, seq):
    tq, tk = B_TQ, B_TK
    assert tq == 2 * tk
    nq = seq // tq
    n_tiles = B_HEADS * HEAD_DIM // LANES
    kern = functools.partial(_stick_kernel, tq=tq, tk=tk)
    return pl.pallas_call(
        kern,
        out_shape=jax.ShapeDtypeStruct((batch * seq, B_HEADS * HEAD_DIM), BF16),
        grid=(batch, n_tiles, nq),
        in_specs=[pl.BlockSpec((tq, LANES), lambda b, g, i: (b * nq + i, g)),
                  pl.BlockSpec((seq, LANES), lambda b, g, i: (b, n_tiles + g)),
                  pl.BlockSpec((seq, LANES), lambda b, g, i: (b, 2 * n_tiles + g))],
        out_specs=pl.BlockSpec((tq, LANES), lambda b, g, i: (b * nq + i, g)),
        scratch_shapes=[pltpu.VMEM((seq, LANES), BF16),
                        pltpu.VMEM((seq // tk, LANES, tk), BF16),
                        pltpu.VMEM((tk, tq), F32),
                        pltpu.VMEM((2, tk, 2 * tq), F32),
                        pltpu.VMEM((1, 2 * tq), F32),
                        pltpu.VMEM((LANES, 2 * tq), F32)],
        compiler_params=_params(("arbitrary", "arbitrary", "arbitrary")),
        name="stick_breaking_attention",
    )(qkv, qkv, qkv)


def _swa_kernel(sink_ref, q_ref, kprev_ref, kcur_ref, vprev_ref, vcur_ref, o_ref, *, tq):
    i = pl.program_id(1)
    w = WINDOW
    qi = lax.broadcasted_iota(jnp.int32, (w, 2 * w), 0)
    ki = lax.broadcasted_iota(jnp.int32, (w, 2 * w), 1)
    rel = w + qi - ki
    band = (rel >= 0) & (rel < w)
    lane = lax.broadcasted_iota(jnp.int32, (w, LANES), 1)
    low = lane < HEAD_DIM
    n_qtiles = C_HEADS * HEAD_DIM // LANES
    tiles_per_group = n_qtiles // C_KV_HEADS
    for sub in range(tq // w):
        rows = slice(sub * w, (sub + 1) * w)
        if sub == 0:
            k_prev, v_prev = kprev_ref[...], vprev_ref[...]
            mask = band & ((ki >= w) | (i > 0))
        else:
            prev = slice((sub - 1) * w, sub * w)
            k_prev, v_prev = kcur_ref[prev, :], vcur_ref[prev, :]
            mask = band
        kwin = jnp.concatenate([k_prev, kcur_ref[rows, :]], axis=0)
        vwin = jnp.concatenate([v_prev, vcur_ref[rows, :]], axis=0)
        for t in range(n_qtiles):
            grp = t // tiles_per_group
            kg = kwin[:, grp * LANES:(grp + 1) * LANES]
            vg = vwin[:, grp * LANES:(grp + 1) * LANES]
            q = q_ref[rows, t * LANES:(t + 1) * LANES]
            zero = jnp.zeros_like(q)
            outs = []
            for c in range(2):
                qc = jnp.where(low, q, zero) if c == 0 else jnp.where(low, zero, q)
                sc = jnp.where(mask, _dot_nt(qc, kg), NEG_BIG)
                sink = sink_ref[2 * t + c] * LOG2E
                mx = jnp.maximum(jnp.max(sc, axis=-1, keepdims=True), sink)
                p = jnp.exp2(sc - mx)
                denom = jnp.sum(p, axis=-1, keepdims=True) + jnp.exp2(sink - mx)
                outs.append(_dot(p.astype(BF16), vg) / denom)
            o_ref[rows, t * LANES:(t + 1) * LANES] = jnp.where(low, outs[0], outs[1]).astype(BF16)


def _swa_attention(qkv, sinks, *, batch, seq):
    tq = C_TQ
    w = WINDOW
    nq = seq // tq
    per = tq // w
    nq_cols = C_HEADS * HEAD_DIM
    kcol = nq_cols // (2 * LANES)
    vcol = kcol + 1

    def prev_map(col):
        return lambda b, i: (jnp.maximum((b * nq + i) * per - 1, 0), col)

    def cur_map(col):
        return lambda b, i: (b * nq + i, col)

    return pl.pallas_call(
        functools.partial(_swa_kernel, tq=tq),
        out_shape=jax.ShapeDtypeStruct((batch * seq, nq_cols), BF16),
        grid=(batch, nq),
        in_specs=[pl.BlockSpec(memory_space=pltpu.SMEM),
                  pl.BlockSpec((tq, nq_cols), lambda b, i: (b * nq + i, 0)),
                  pl.BlockSpec((w, 2 * LANES), prev_map(kcol)),
                  pl.BlockSpec((tq, 2 * LANES), cur_map(kcol)),
                  pl.BlockSpec((w, 2 * LANES), prev_map(vcol)),
                  pl.BlockSpec((tq, 2 * LANES), cur_map(vcol))],
        out_specs=pl.BlockSpec((tq, nq_cols), lambda b, i: (b * nq + i, 0)),
        compiler_params=_params(("arbitrary", "arbitrary")),
        name="sliding_window_attention",
    )(sinks, qkv, qkv, qkv, qkv, qkv)


def kernel(x, positions, norm_gains, a_w_in, a_w_out, a_lambda, a_subln, b_w_in, b_w_out,
           c_w_in, c_w_out, c_sinks, ffn_w_gate, ffn_w_up, ffn_w_down):
    batch, seq, d = x.shape
    depth = norm_gains.shape[0]
    tables = _rope_tables(positions)
    h = x.reshape(batch * seq, d)
    for i in range(depth):
        kind = i % N_MIXERS
        inst = i // N_MIXERS
        g = norm_gains[i]
        if kind == 0:
            qkv = _in_proj(h, g[0], a_w_in[inst].astype(BF16), tables, n_rope=16, n_q=8,
                           q_scale=ATTN_SCALE * LOG2E)
            mix = _diff_attention(qkv, a_lambda[inst], a_subln[inst], batch=batch, seq=seq, layer_idx=i)
            w_out = a_w_out[inst]
        elif kind == 1:
            qkv = _in_proj(h, g[0], b_w_in[inst].astype(BF16), None, n_rope=0, n_q=8,
                           q_scale=ATTN_SCALE * 0.5)
            mix = _stick_attention(qkv, batch=batch, seq=seq)
            w_out = b_w_out[inst]
        else:
            w = c_w_in[inst]
            nq_cols = C_HEADS * HEAD_DIM
            kv = w[:, nq_cols:].reshape(d, 2 * C_KV_HEADS, 1, HEAD_DIM)
            kv = jnp.broadcast_to(kv, (d, 2 * C_KV_HEADS, 2, HEAD_DIM)).reshape(d, 4 * C_KV_HEADS * HEAD_DIM)
            w = jnp.concatenate([w[:, :nq_cols], kv], axis=1).astype(BF16)
            qkv = _in_proj(h, g[0], w, tables, n_rope=10, n_q=8, q_scale=ATTN_SCALE * LOG2E)
            mix = _swa_attention(qkv, c_sinks[inst], batch=batch, seq=seq)
            w_out = c_w_out[inst]
        h = _post_mixer(mix, w_out.astype(BF16), g[1], h, g[2], ffn_w_gate[i].astype(BF16),
                        ffn_w_up[i].astype(BF16), ffn_w_down[i].astype(BF16), g[3])
    return h.reshape(batch, seq, d)
```

```python
import functools
import math

import jax
import jax.numpy as jnp
from jax import lax
from jax.experimental import pallas as pl
from jax.experimental.pallas import tpu as pltpu

F32 = jnp.float32
BF16 = jnp.bfloat16

D_MODEL = 1024
N_MIXERS = 3
ROPE_THETA = 10000.0
ROPE_DIM = 64
NORM_EPS = 1e-6
HEAD_DIM = 64
ATTN_SCALE = HEAD_DIM ** -0.5
A_HEADS = 8
B_HEADS = 16
C_HEADS = 16
C_KV_HEADS = 2
WINDOW = 128

LANES = 128
NEG_BIG = -1e30
NEGLIGIBLE = 2.0 ** -100
VMEM_LIMIT = 56 * 1024 * 1024

ROW_TILE = 1024
FFN_PARTS = 2
A_TQ = 1024
A_TK = 256
B_TQ = 512
B_TK = 256
C_TQ = 512
LOG2E = math.log2(math.e)
SUM_ROWS = 16

def _params(sem):
    return pltpu.CompilerParams(dimension_semantics=sem, vmem_limit_bytes=VMEM_LIMIT)


def _rms(x, g):
    ms = jnp.mean(x * x, axis=-1, keepdims=True)
    return x * lax.rsqrt(ms + NORM_EPS) * g


def _dot(a, b):
    return jnp.dot(a, b, preferred_element_type=F32)


def _dot_nt(a, b):
    return lax.dot_general(a, b, (((1,), (1,)), ((), ())), preferred_element_type=F32)


def _halves_side_by_side(q, blk):
    lane = lax.broadcasted_iota(jnp.int32, q.shape, 1)
    zero = jnp.zeros_like(q)
    first = jnp.where(lane < HEAD_DIM, q, zero)
    second = jnp.where(lane >= HEAD_DIM, q, zero)
    parts = []
    for g in range(q.shape[0] // blk):
        rows = slice(g * blk, (g + 1) * blk)
        parts += [first[rows], second[rows]]
    return jnp.concatenate(parts, axis=0)


def _split_side_by_side(x, blk):
    groups = x.shape[1] // (2 * blk)
    first = [x[:, 2 * g * blk:(2 * g + 1) * blk] for g in range(groups)]
    second = [x[:, (2 * g + 1) * blk:(2 * g + 2) * blk] for g in range(groups)]
    return jnp.concatenate(first, axis=1), jnp.concatenate(second, axis=1)


def _rope_table_kernel(pos_ref, inv_ref, sign_ref, cos_ref, sin_ref):
    ang = pos_ref[...].astype(F32) * inv_ref[...]
    cos_ref[...] = jnp.cos(ang)
    sin_ref[...] = jnp.sin(ang) * sign_ref[...]


def _rope_tables(positions):
    m = positions.size
    inv = ROPE_THETA ** (-jnp.arange(0, ROPE_DIM, 2, dtype=F32) / ROPE_DIM)
    inv128 = jnp.tile(inv, LANES // (ROPE_DIM // 2)).reshape(1, LANES)
    half = ROPE_DIM // 2
    sign = jnp.where((jnp.arange(LANES) % ROPE_DIM) < half, -1.0, 1.0).astype(F32).reshape(1, LANES)
    tm = 2048
    return pl.pallas_call(
        _rope_table_kernel,
        out_shape=(jax.ShapeDtypeStruct((m, LANES), F32), jax.ShapeDtypeStruct((m, LANES), F32)),
        grid=(m // tm,),
        in_specs=[pl.BlockSpec((tm, 1), lambda i: (i, 0)),
                  pl.BlockSpec((1, LANES), lambda i: (0, 0)),
                  pl.BlockSpec((1, LANES), lambda i: (0, 0))],
        out_specs=(pl.BlockSpec((tm, LANES), lambda i: (i, 0)),
                   pl.BlockSpec((tm, LANES), lambda i: (i, 0))),
        compiler_params=_params(("arbitrary",)),
        name="rope_tables",
    )(positions.reshape(m, 1), inv128, sign)


def _swap_halves(x):
    lane = lax.broadcasted_iota(jnp.int32, x.shape, 1)
    fwd = pltpu.roll(x, ROPE_DIM // 2, 1)
    bwd = pltpu.roll(x, LANES - ROPE_DIM // 2, 1)
    return jnp.where((lane & (ROPE_DIM - 1)) < ROPE_DIM // 2, bwd, fwd)


def _in_proj_kernel(*refs, n_tiles, n_rope, n_q, q_scale, chunk):
    if n_rope:
        x_ref, g_ref, w_ref, cos_ref, sin_ref, o_ref = refs
    else:
        x_ref, g_ref, w_ref, o_ref = refs
    half = x_ref.shape[0] // 2
    parts = [slice(r * half, (r + 1) * half) for r in range(2)]
    hns = [_rms(x_ref[rows, :], g_ref[...]).astype(BF16) for rows in parts]
    tiles_per_chunk = chunk // LANES
    for rows, hn in zip(parts, hns):
        if n_rope:
            cos = cos_ref[rows, :]
            sin = sin_ref[rows, :]
        for c in range(n_tiles // tiles_per_chunk):
            acc = _dot(hn, w_ref[:, c * chunk:(c + 1) * chunk])
            for t in range(tiles_per_chunk):
                tile = c * tiles_per_chunk + t
                a = acc[:, t * LANES:(t + 1) * LANES]
                if tile < n_rope:
                    a = a * cos + _swap_halves(a) * sin
                if tile < n_q:
                    a = a * q_scale
                o_ref[rows, tile * LANES:(tile + 1) * LANES] = a.astype(BF16)


def _in_proj(x, g, w, tables, *, n_rope, n_q, q_scale):
    m = x.shape[0]
    n = w.shape[1]
    tm = ROW_TILE
    chunk = 256
    kern = functools.partial(_in_proj_kernel, n_tiles=n // LANES, n_rope=n_rope, n_q=n_q,
                             q_scale=q_scale, chunk=chunk)
    in_specs = [pl.BlockSpec((tm, D_MODEL), lambda i: (i, 0)),
                pl.BlockSpec((1, D_MODEL), lambda i: (0, 0)),
                pl.BlockSpec((D_MODEL, n), lambda i: (0, 0))]
    args = [x, g.reshape(1, D_MODEL), w]
    if n_rope:
        in_specs += [pl.BlockSpec((tm, LANES), lambda i: (i, 0)),
                     pl.BlockSpec((tm, LANES), lambda i: (i, 0))]
        args += list(tables)
    return pl.pallas_call(
        kern,
        out_shape=jax.ShapeDtypeStruct((m, n), BF16),
        grid=(m // tm,),
        in_specs=in_specs,
        out_specs=pl.BlockSpec((tm, n), lambda i: (i, 0)),
        compiler_params=_params(("arbitrary",)),
        name="in_proj",
    )(*args)


def _post_mixer_kernel(mix_ref, wo_ref, gmix_ref, x_ref, gin_ref, wg_ref, wu_ref, wd_ref, gout_ref,
                       o_ref, act_ref, *, chunk):
    hidden = wg_ref.shape[1]
    part = x_ref.shape[0] // FFN_PARTS
    parts = [slice(r * part, (r + 1) * part) for r in range(FFN_PARTS)]
    xs, hns = [], []
    for rows in parts:
        x = x_ref[rows, :] + _rms(_dot(mix_ref[rows, :], wo_ref[...]), gmix_ref[...])
        xs.append(x)
        hns.append(_rms(x, gin_ref[...]).astype(BF16))
    for rows, hn in zip(parts, hns):
        for c in range(hidden // chunk):
            cols = slice(c * chunk, (c + 1) * chunk)
            gate = _dot(hn, wg_ref[:, cols])
            up = _dot(hn, wu_ref[:, cols])
            act_ref[rows, cols] = (gate * jax.nn.sigmoid(gate) * up).astype(BF16)
    for rows, x in zip(parts, xs):
        f = _dot(act_ref[rows, :], wd_ref[...])
        o_ref[rows, :] = x + _rms(f, gout_ref[...])


def _post_mixer(mix, wo, gmix, x, gin, wg, wu, wd, gout):
    m = x.shape[0]
    hidden = wg.shape[1]
    tm = ROW_TILE
    const = lambda i: (0, 0)
    rows = lambda i: (i, 0)
    resident = functools.partial(pl.BlockSpec, index_map=const, pipeline_mode=pl.Buffered(1))
    gain = pl.BlockSpec((1, D_MODEL), const)
    return pl.pallas_call(
        functools.partial(_post_mixer_kernel, chunk=256),
        out_shape=jax.ShapeDtypeStruct((m, D_MODEL), F32),
        grid=(m // tm,),
        in_specs=[pl.BlockSpec((tm, D_MODEL), rows),
                  resident((D_MODEL, D_MODEL)),
                  gain,
                  pl.BlockSpec((tm, D_MODEL), rows),
                  gain,
                  resident((D_MODEL, hidden)),
                  resident((D_MODEL, hidden)),
                  resident((hidden, D_MODEL)),
                  gain],
        out_specs=pl.BlockSpec((tm, D_MODEL), rows),
        scratch_shapes=[pltpu.VMEM((tm, hidden), BF16)],
        compiler_params=_params(("arbitrary",)),
        name="out_proj_ffn",
    )(mix, wo, gmix.reshape(1, D_MODEL), x, gin.reshape(1, D_MODEL), wg, wu, wd,
      gout.reshape(1, D_MODEL))


def _diff_attn_kernel(lam_ref, subln_ref, q_ref, k_ref, v_ref, o_ref,
                      vt_ref, tri_ref, s_ref, bmax_ref, m_ref, acc_ref, *, tq, tk, lam_init):
    i = pl.program_id(2)
    n = 2 * tq

    @pl.when(i == 0)
    def _():
        ones_row = jnp.where(lax.broadcasted_iota(jnp.int32, (SUM_ROWS, tk), 0) == 0, 1.0, 0.0)
        for c in range(vt_ref.shape[0]):
            vt_ref[c, 0:LANES, :] = v_ref[c * tk:(c + 1) * tk, :].astype(F32).T.astype(BF16)
            vt_ref[c, LANES:, :] = ones_row.astype(BF16)
        key = lax.broadcasted_iota(jnp.int32, tri_ref.shape, 0)
        qry = lax.broadcasted_iota(jnp.int32, tri_ref.shape, 1) & (tk - 1)
        tri_ref[...] = jnp.where(key <= qry, 0.0, NEG_BIG)

    def query_columns(step):
        q = q_ref[pl.ds(pl.multiple_of(step * tq, tq), tq), :]
        return _halves_side_by_side(q, tk)

    qcat = query_columns(i)

    m_ref[...] = jnp.full(m_ref.shape, NEG_BIG, F32)
    acc_ref[...] = jnp.zeros(acc_ref.shape, F32)

    def scores(j, lo=0):
        kb = k_ref[pl.ds(pl.multiple_of(j * tk, tk), tk), :]
        return _dot_nt(kb, qcat[lo:])

    def fill(slot, s):
        s_ref[slot] = s
        bmax_ref[slot] = jnp.max(s, axis=0, keepdims=True)

    @pl.when(i == 0)
    def _():
        fill(0, scores(0))

    def update(s, j, lo=0, bmax=None):
        m_old = m_ref[:, lo:]
        if bmax is None:
            bmax = jnp.max(s, axis=0, keepdims=True)
        m_new = jnp.maximum(m_old, bmax)
        alpha = jnp.exp2(m_old - m_new)
        p = jnp.exp2(s - m_new).astype(BF16)
        acc_ref[:, lo:] = alpha * acc_ref[:, lo:] + _dot(vt_ref[j], p)
        m_ref[:, lo:] = m_new

    def pair(j):
        fill(1, scores(j + 1))
        update(s_ref[0], j, bmax=bmax_ref[0])
        fill(0, scores(j + 2))
        update(s_ref[1], j + 1, bmax=bmax_ref[1])

    def body(t, carry):
        pair(4 * t)
        pair(4 * t + 2)
        return carry

    ratio = tq // tk
    below = ratio * i
    lax.fori_loop(0, below >> 2, body, 0)

    if ratio % 4:
        @pl.when((below & 3) == 2)
        def _():
            pair(below - 2)

    tri = tri_ref[...]
    group = 2 * tk
    for d in range(ratio):
        lo = d * group
        if d + 1 < ratio:
            s_ref[(d + 1) % 2, :, lo + group:] = scores(below + d + 1, lo + group)
        s = s_ref[d % 2, :, lo:]
        if d + 1 < ratio:
            s = jnp.concatenate([s[:, :group] + tri, s[:, group:]], axis=1)
        else:
            s = s + tri
        update(s, below + d, lo)

    nxt = jnp.minimum(i + 1, pl.num_programs(2) - 1)
    fill(0, _dot_nt(k_ref[0:tk, :], query_columns(nxt)))

    lp = lam_ref[...]
    lam = (jnp.exp(jnp.sum(lp[0:1] * lp[1:2], axis=-1, keepdims=True))
           - jnp.exp(jnp.sum(lp[2:3] * lp[3:4], axis=-1, keepdims=True)) + lam_init)
    acc = acc_ref[...]
    first, second = _split_side_by_side(acc[:LANES] / acc[LANES:LANES + 1], tk)
    o_t = first - lam * second
    ms = jnp.mean(o_t * o_t, axis=0, keepdims=True)
    o_t = o_t * lax.rsqrt(ms + NORM_EPS) * (subln_ref[...] * (1.0 - lam_init))
    o_ref[...] = o_t.T.astype(BF16)


def _diff_attention(qkv, lam_params, subln_g, *, batch, seq, layer_idx):
    tq, tk = A_TQ, A_TK
    assert tq % (2 * tk) == 0
    nq = seq // tq
    lam_init = 0.8 - 0.6 * math.exp(-0.3 * layer_idx)
    kern = functools.partial(_diff_attn_kernel, tq=tq, tk=tk, lam_init=lam_init)
    return pl.pallas_call(
        kern,
        out_shape=jax.ShapeDtypeStruct((batch * seq, A_HEADS * LANES), BF16),
        grid=(batch, A_HEADS, nq),
        in_specs=[pl.BlockSpec((4, HEAD_DIM), lambda b, h, i: (0, 0)),
                  pl.BlockSpec((LANES, 1), lambda b, h, i: (0, 0)),
                  pl.BlockSpec((seq, LANES), lambda b, h, i: (b, h)),
                  pl.BlockSpec((seq, LANES), lambda b, h, i: (b, A_HEADS + h)),
                  pl.BlockSpec((seq, LANES), lambda b, h, i: (b, 2 * A_HEADS + h))],
        out_specs=pl.BlockSpec((tq, LANES), lambda b, h, i: (b * nq + i, h)),
        scratch_shapes=[pltpu.VMEM((seq // tk, LANES + SUM_ROWS, tk), BF16),
                        pltpu.VMEM((tk, 2 * tk), F32),
                        pltpu.VMEM((2, tk, 2 * tq), F32),
                        pltpu.VMEM((2, 1, 2 * tq), F32),
                        pltpu.VMEM((1, 2 * tq), F32),
                        pltpu.VMEM((LANES + SUM_ROWS, 2 * tq), F32)],
        compiler_params=_params(("arbitrary", "arbitrary", "arbitrary")),
        name="diff_attention",
    )(lam_params, subln_g.reshape(LANES, 1), qkv, qkv, qkv)


def _stick_kernel(q_ref, k_ref, v_ref, o_ref, kp_ref, vt_ref, tri_ref, s_ref, carry_ref, acc_ref,
                  *, tq, tk):
    i = pl.program_id(2)
    n = 2 * tq
    nv = tk // 8

    @pl.when(i == 0)
    def _():
        r = lax.broadcasted_iota(jnp.int32, (tk, tk), 0)
        c = lax.broadcasted_iota(jnp.int32, (tk, tk), 1)
        perm = jnp.where(c == (r & 7) * nv + (r >> 3), 1.0, 0.0).astype(BF16)

        for blk in range(vt_ref.shape[0]):
            rows = slice(blk * tk, (blk + 1) * tk)
            kv = _dot(perm, jnp.concatenate([k_ref[rows, :], v_ref[rows, :]], axis=1))
            kp_ref[rows, :] = kv[:, :LANES].astype(BF16)
            vt_ref[blk] = kv[:, LANES:].T.astype(BF16)
        row = lax.broadcasted_iota(jnp.int32, (tk, tq), 0)
        kpos = (row & 7) * nv + (row >> 3)
        qry = lax.broadcasted_iota(jnp.int32, (tk, tq), 1) & (tk - 1)
        tri_ref[...] = jnp.where(kpos < qry, 0.0, NEG_BIG)

    qcat = _halves_side_by_side(q_ref[...], tk)

    carry_ref[...] = jnp.ones(carry_ref.shape, F32)
    acc_ref[...] = jnp.zeros(acc_ref.shape, F32)

    def scores(j, lo=0):
        kb = kp_ref[pl.ds(pl.multiple_of(j * tk, tk), tk), :]
        return _dot_nt(kb, qcat[lo:])


    def update(zh, j, lo=0):
        w = n - lo
        beta = 0.5 + 0.5 * jnp.tanh(zh)
        run = jnp.ones((8, w), F32)
        bw = [None] * nv
        for v in reversed(range(nv)):
            rows = slice(v * 8, (v + 1) * 8)
            bw[v] = beta[rows] * run
            run = run - bw[v]
        sub = lax.broadcasted_iota(jnp.int32, (8, w), 0)
        later = jnp.ones((8, w), F32)
        for sp in range(7, 0, -1):
            later = jnp.where(sub < sp, later * run[sp:sp + 1, :], later)
        carry = carry_ref[:, lo:]
        scale = later * carry
        a = jnp.concatenate([bw[v] * scale for v in range(nv)], axis=0).astype(BF16)
        acc_ref[:, lo:] += _dot(vt_ref[j], a)
        carry_ref[:, lo:] = carry * (later[0:1, :] * run[0:1, :])

    top = 2 * i + 1

    def block(j):
        return jnp.maximum(j, 0)

    def absent(j):
        return jnp.where(j < 0, NEG_BIG, 0.0)

    s_ref[0, :, tq:] = scores(top, tq)
    s_ref[1] = scores(top - 1)
    tri = tri_ref[...]
    update(s_ref[0, :, tq:] + tri, top, tq)
    s_ref[0] = scores(block(top - 2))
    update(jnp.concatenate([s_ref[1, :, :tq] + tri, s_ref[1, :, tq:]], axis=1), top - 1)
    s_ref[1] = scores(block(top - 3))
    update(s_ref[0] + absent(top - 2), block(top - 2))

    def live():
        return jnp.max(carry_ref[...]) > NEGLIGIBLE

    def cond(state):
        t, alive = state
        return jnp.logical_and(t < i, alive)

    def body(state):
        t, _ = state
        j = top - 3 - 2 * t
        s_ref[0] = scores(block(j - 1))
        update(s_ref[1], j)
        s_ref[1] = scores(block(j - 2))
        update(s_ref[0] + absent(j - 1), block(j - 1))
        return t + 1, live()

    lax.while_loop(cond, body, (0, live()))
    even_head, odd_head = _split_side_by_side(acc_ref[...], tk)
    head_row = lax.broadcasted_iota(jnp.int32, (LANES, tq), 0) < HEAD_DIM
    o_t = jnp.where(head_row, even_head, odd_head)
    o_ref[...] = o_t.T.astype(BF16)


def _stick_attention(qkv, *, batch, seq):
    tq, tk = B_TQ, B_TK
    assert tq == 2 * tk
    nq = seq // tq
    n_tiles = B_HEADS * HEAD_DIM // LANES
    kern = functools.partial(_stick_kernel, tq=tq, tk=tk)
    return pl.pallas_call(
        kern,
        out_shape=jax.ShapeDtypeStruct((batch * seq, B_HEADS * HEAD_DIM), BF16),
        grid=(batch, n_tiles, nq),
        in_specs=[pl.BlockSpec((tq, LANES), lambda b, g, i: (b * nq + i, g)),
                  pl.BlockSpec((seq, LANES), lambda b, g, i: (b, n_tiles + g)),
                  pl.BlockSpec((seq, LANES), lambda b, g, i: (b, 2 * n_tiles + g))],
        out_specs=pl.BlockSpec((tq, LANES), lambda b, g, i: (b * nq + i, g)),
        scratch_shapes=[pltpu.VMEM((seq, LANES), BF16),
                        pltpu.VMEM((seq // tk, LANES, tk), BF16),
                        pltpu.VMEM((tk, tq), F32),
                        pltpu.VMEM((2, tk, 2 * tq), F32),
                        pltpu.VMEM((1, 2 * tq), F32),
                        pltpu.VMEM((LANES, 2 * tq), F32)],
        compiler_params=_params(("arbitrary", "arbitrary", "arbitrary")),
        name="stick_breaking_attention",
    )(qkv, qkv, qkv)


def _swa_kernel(sink_ref, q_ref, kprev_ref, kcur_ref, vprev_ref, vcur_ref, o_ref, *, tq):
    i = pl.program_id(1)
    w = WINDOW
    qi = lax.broadcasted_iota(jnp.int32, (w, 2 * w), 0)
    ki = lax.broadcasted_iota(jnp.int32, (w, 2 * w), 1)
    rel = w + qi - ki
    band = (rel >= 0) & (rel < w)
    lane = lax.broadcasted_iota(jnp.int32, (w, LANES), 1)
    low = lane < HEAD_DIM
    n_qtiles = C_HEADS * HEAD_DIM // LANES
    tiles_per_group = n_qtiles // C_KV_HEADS
    for sub in range(tq // w):
        rows = slice(sub * w, (sub + 1) * w)
        if sub == 0:
            k_prev, v_prev = kprev_ref[...], vprev_ref[...]
            mask = band & ((ki >= w) | (i > 0))
        else:
            prev = slice((sub - 1) * w, sub * w)
            k_prev, v_prev = kcur_ref[prev, :], vcur_ref[prev, :]
            mask = band
        kwin = jnp.concatenate([k_prev, kcur_ref[rows, :]], axis=0)
        vwin = jnp.concatenate([v_prev, vcur_ref[rows, :]], axis=0)
        for t in range(n_qtiles):
            grp = t // tiles_per_group
            kg = kwin[:, grp * LANES:(grp + 1) * LANES]
            vg = vwin[:, grp * LANES:(grp + 1) * LANES]
            q = q_ref[rows, t * LANES:(t + 1) * LANES]
            zero = jnp.zeros_like(q)
            outs = []
            for c in range(2):
                qc = jnp.where(low, q, zero) if c == 0 else jnp.where(low, zero, q)
                sc = jnp.where(mask, _dot_nt(qc, kg), NEG_BIG)
                sink = sink_ref[2 * t + c] * LOG2E
                mx = jnp.maximum(jnp.max(sc, axis=-1, keepdims=True), sink)
                p = jnp.exp2(sc - mx)
                denom = jnp.sum(p, axis=-1, keepdims=True) + jnp.exp2(sink - mx)
                outs.append(_dot(p.astype(BF16), vg) / denom)
            o_ref[rows, t * LANES:(t + 1) * LANES] = jnp.where(low, outs[0], outs[1]).astype(BF16)


def _swa_attention(qkv, sinks, *, batch, seq):
    tq = C_TQ
    w = WINDOW
    nq = seq // tq
    per = tq // w
    nq_cols = C_HEADS * HEAD_DIM
    kcol = nq_cols // (2 * LANES)
    vcol = kcol + 1

    def prev_map(col):
        return lambda b, i: (jnp.maximum((b * nq + i) * per - 1, 0), col)

    def cur_map(col):
        return lambda b, i: (b * nq + i, col)

    return pl.pallas_call(
        functools.partial(_swa_kernel, tq=tq),
        out_shape=jax.ShapeDtypeStruct((batch * seq, nq_cols), BF16),
        grid=(batch, nq),
        in_specs=[pl.BlockSpec(memory_space=pltpu.SMEM),
                  pl.BlockSpec((tq, nq_cols), lambda b, i: (b * nq + i, 0)),
                  pl.BlockSpec((w, 2 * LANES), prev_map(kcol)),
                  pl.BlockSpec((tq, 2 * LANES), cur_map(kcol)),
                  pl.BlockSpec((w, 2 * LANES), prev_map(vcol)),
                  pl.BlockSpec((tq, 2 * LANES), cur_map(vcol))],
        out_specs=pl.BlockSpec((tq, nq_cols), lambda b, i: (b * nq + i, 0)),
        compiler_params=_params(("arbitrary", "arbitrary")),
        name="sliding_window_attention",
    )(sinks, qkv, qkv, qkv, qkv, qkv)


def kernel(x, positions, norm_gains, a_w_in, a_w_out, a_lambda, a_subln, b_w_in, b_w_out,
           c_w_in, c_w_out, c_sinks, ffn_w_gate, ffn_w_up, ffn_w_down):
    batch, seq, d = x.shape
    depth = norm_gains.shape[0]
    tables = _rope_tables(positions)
    h = x.reshape(batch * seq, d)
    for i in range(depth):
        kind = i % N_MIXERS
        inst = i // N_MIXERS
        g = norm_gains[i]
        if kind == 0:
            qkv = _in_proj(h, g[0], a_w_in[inst].astype(BF16), tables, n_rope=16, n_q=8,
                           q_scale=ATTN_SCALE * LOG2E)
            mix = _diff_attention(qkv, a_lambda[inst], a_subln[inst], batch=batch, seq=seq, layer_idx=i)
            w_out = a_w_out[inst]
        elif kind == 1:
            qkv = _in_proj(h, g[0], b_w_in[inst].astype(BF16), None, n_rope=0, n_q=8,
                           q_scale=ATTN_SCALE * 0.5)
            mix = _stick_attention(qkv, batch=batch, seq=seq)
            w_out = b_w_out[inst]
        else:
            w = c_w_in[inst]
            nq_cols = C_HEADS * HEAD_DIM
            kv = w[:, nq_cols:].reshape(d, 2 * C_KV_HEADS, 1, HEAD_DIM)
            kv = jnp.broadcast_to(kv, (d, 2 * C_KV_HEADS, 2, HEAD_DIM)).reshape(d, 4 * C_KV_HEADS * HEAD_DIM)
            w = jnp.concatenate([w[:, :nq_cols], kv], axis=1).astype(BF16)
            qkv = _in_proj(h, g[0], w, tables, n_rope=10, n_q=8, q_scale=ATTN_SCALE * LOG2E)
            mix = _swa_attention(qkv, c_sinks[inst], batch=batch, seq=seq)
            w_out = c_w_out[inst]
        h = _post_mixer(mix, w_out.astype(BF16), g[1], h, g[2], ffn_w_gate[i].astype(BF16),
                        ffn_w_up[i].astype(BF16), ffn_w_down[i].astype(BF16), g[3])
    return h.reshape(batch, seq, d)
```

```python
import functools
import math

import jax
import jax.numpy as jnp
from jax import lax
from jax.experimental import pallas as pl
from jax.experimental.pallas import tpu as pltpu

F32 = jnp.float32
BF16 = jnp.bfloat16

D_MODEL = 1024
N_MIXERS = 3
ROPE_THETA = 10000.0
ROPE_DIM = 64
NORM_EPS = 1e-6
HEAD_DIM = 64
ATTN_SCALE = HEAD_DIM ** -0.5
A_HEADS = 8
B_HEADS = 16
C_HEADS = 16
C_KV_HEADS = 2
WINDOW = 128

LANES = 128
NEG_BIG = -1e30
NEGLIGIBLE = 2.0 ** -100
VMEM_LIMIT = 56 * 1024 * 1024

ROW_TILE = 1024
FFN_PARTS = 2
A_TQ = 1024
A_TK = 256
B_TQ = 512
B_TK = 256
C_TQ = 512
LOG2E = math.log2(math.e)
SUM_ROWS = 16

def _params(sem):
    return pltpu.CompilerParams(dimension_semantics=sem, vmem_limit_bytes=VMEM_LIMIT)


def _rms(x, g):
    ms = jnp.mean(x * x, axis=-1, keepdims=True)
    return x * lax.rsqrt(ms + NORM_EPS) * g


def _dot(a, b):
    return jnp.dot(a, b, preferred_element_type=F32)


def _dot_nt(a, b):
    return lax.dot_general(a, b, (((1,), (1,)), ((), ())), preferred_element_type=F32)


def _halves_side_by_side(q, blk):
    lane = lax.broadcasted_iota(jnp.int32, q.shape, 1)
    zero = jnp.zeros_like(q)
    first = jnp.where(lane < HEAD_DIM, q, zero)
    second = jnp.where(lane >= HEAD_DIM, q, zero)
    parts = []
    for g in range(q.shape[0] // blk):
        rows = slice(g * blk, (g + 1) * blk)
        parts += [first[rows], second[rows]]
    return jnp.concatenate(parts, axis=0)


def _split_side_by_side(x, blk):
    groups = x.shape[1] // (2 * blk)
    first = [x[:, 2 * g * blk:(2 * g + 1) * blk] for g in range(groups)]
    second = [x[:, (2 * g + 1) * blk:(2 * g + 2) * blk] for g in range(groups)]
    return jnp.concatenate(first, axis=1), jnp.concatenate(second, axis=1)


def _rope_table_kernel(pos_ref, inv_ref, sign_ref, cos_ref, sin_ref):
    ang = pos_ref[...].astype(F32) * inv_ref[...]
    cos_ref[...] = jnp.cos(ang)
    sin_ref[...] = jnp.sin(ang) * sign_ref[...]


def _rope_tables(positions):
    m = positions.size
    inv = ROPE_THETA ** (-jnp.arange(0, ROPE_DIM, 2, dtype=F32) / ROPE_DIM)
    inv128 = jnp.tile(inv, LANES // (ROPE_DIM // 2)).reshape(1, LANES)
    half = ROPE_DIM // 2
    sign = jnp.where((jnp.arange(LANES) % ROPE_DIM) < half, -1.0, 1.0).astype(F32).reshape(1, LANES)
    tm = 2048
    return pl.pallas_call(
        _rope_table_kernel,
        out_shape=(jax.ShapeDtypeStruct((m, LANES), F32), jax.ShapeDtypeStruct((m, LANES), F32)),
        grid=(m // tm,),
        in_specs=[pl.BlockSpec((tm, 1), lambda i: (i, 0)),
                  pl.BlockSpec((1, LANES), lambda i: (0, 0)),
                  pl.BlockSpec((1, LANES), lambda i: (0, 0))],
        out_specs=(pl.BlockSpec((tm, LANES), lambda i: (i, 0)),
                   pl.BlockSpec((tm, LANES), lambda i: (i, 0))),
        compiler_params=_params(("arbitrary",)),
        name="rope_tables",
    )(positions.reshape(m, 1), inv128, sign)


def _swap_halves(x):
    lane = lax.broadcasted_iota(jnp.int32, x.shape, 1)
    fwd = pltpu.roll(x, ROPE_DIM // 2, 1)
    bwd = pltpu.roll(x, LANES - ROPE_DIM // 2, 1)
    return jnp.where((lane & (ROPE_DIM - 1)) < ROPE_DIM // 2, bwd, fwd)


def _in_proj_kernel(*refs, n_tiles, n_rope, n_q, q_scale, chunk):
    if n_rope:
        x_ref, g_ref, w_ref, cos_ref, sin_ref, o_ref = refs
    else:
        x_ref, g_ref, w_ref, o_ref = refs
    half = x_ref.shape[0] // 2
    parts = [slice(r * half, (r + 1) * half) for r in range(2)]
    hns = [_rms(x_ref[rows, :], g_ref[...]).astype(BF16) for rows in parts]
    tiles_per_chunk = chunk // LANES
    for rows, hn in zip(parts, hns):
        if n_rope:
            cos = cos_ref[rows, :]
            sin = sin_ref[rows, :]
        for c in range(n_tiles // tiles_per_chunk):
            acc = _dot(hn, w_ref[:, c * chunk:(c + 1) * chunk])
            for t in range(tiles_per_chunk):
                tile = c * tiles_per_chunk + t
                a = acc[:, t * LANES:(t + 1) * LANES]
                if tile < n_rope:
                    a = a * cos + _swap_halves(a) * sin
                if tile < n_q:
                    a = a * q_scale
                o_ref[rows, tile * LANES:(tile + 1) * LANES] = a.astype(BF16)


def _in_proj(x, g, w_stack, idx, tables, *, n_rope, n_q, q_scale):
    m = x.shape[0]
    n = w_stack.shape[2]
    tm = ROW_TILE
    chunk = 256
    kern = functools.partial(_in_proj_kernel, n_tiles=n // LANES, n_rope=n_rope, n_q=n_q,
                             q_scale=q_scale, chunk=chunk)
    in_specs = [pl.BlockSpec((tm, D_MODEL), lambda i: (i, 0)),
                pl.BlockSpec((1, D_MODEL), lambda i: (0, 0)),
                pl.BlockSpec((None, D_MODEL, n), lambda i: (idx, 0, 0))]
    args = [x, g.reshape(1, D_MODEL), w_stack]
    if n_rope:
        in_specs += [pl.BlockSpec((tm, LANES), lambda i: (i, 0)),
                     pl.BlockSpec((tm, LANES), lambda i: (i, 0))]
        args += list(tables)
    return pl.pallas_call(
        kern,
        out_shape=jax.ShapeDtypeStruct((m, n), BF16),
        grid=(m // tm,),
        in_specs=in_specs,
        out_specs=pl.BlockSpec((tm, n), lambda i: (i, 0)),
        compiler_params=_params(("arbitrary",)),
        name="in_proj",
    )(*args)


def _post_mixer_kernel(mix_ref, wo_ref, gmix_ref, x_ref, gin_ref, wg_ref, wu_ref, wd_ref, gout_ref,
                       o_ref, act_ref, *, chunk):
    hidden = wg_ref.shape[1]
    part = x_ref.shape[0] // FFN_PARTS
    parts = [slice(r * part, (r + 1) * part) for r in range(FFN_PARTS)]
    xs, hns = [], []
    for rows in parts:
        x = x_ref[rows, :] + _rms(_dot(mix_ref[rows, :], wo_ref[...]), gmix_ref[...])
        xs.append(x)
        hns.append(_rms(x, gin_ref[...]).astype(BF16))
    for rows, hn in zip(parts, hns):
        for c in range(hidden // chunk):
            cols = slice(c * chunk, (c + 1) * chunk)
            gate = _dot(hn, wg_ref[:, cols])
            up = _dot(hn, wu_ref[:, cols])
            act_ref[rows, cols] = (gate * jax.nn.sigmoid(gate) * up).astype(BF16)
    for rows, x in zip(parts, xs):
        f = _dot(act_ref[rows, :], wd_ref[...])
        o_ref[rows, :] = x + _rms(f, gout_ref[...])


def _post_mixer(mix, wo, wo_idx, gmix, x, gin, wg, wu, wd, layer, gout):
    m = x.shape[0]
    hidden = wg.shape[2]
    tm = ROW_TILE
    const = lambda i: (0, 0)
    rows = lambda i: (i, 0)

    def resident(shape, which):
        return pl.BlockSpec((None,) + shape, lambda i: (which, 0, 0), pipeline_mode=pl.Buffered(1))

    gain = pl.BlockSpec((1, D_MODEL), const)
    return pl.pallas_call(
        functools.partial(_post_mixer_kernel, chunk=256),
        out_shape=jax.ShapeDtypeStruct((m, D_MODEL), F32),
        grid=(m // tm,),
        in_specs=[pl.BlockSpec((tm, D_MODEL), rows),
                  resident((D_MODEL, D_MODEL), wo_idx),
                  gain,
                  pl.BlockSpec((tm, D_MODEL), rows),
                  gain,
                  resident((D_MODEL, hidden), layer),
                  resident((D_MODEL, hidden), layer),
                  resident((hidden, D_MODEL), layer),
                  gain],
        out_specs=pl.BlockSpec((tm, D_MODEL), rows),
        scratch_shapes=[pltpu.VMEM((tm, hidden), BF16)],
        compiler_params=_params(("arbitrary",)),
        name="out_proj_ffn",
    )(mix, wo, gmix.reshape(1, D_MODEL), x, gin.reshape(1, D_MODEL), wg, wu, wd,
      gout.reshape(1, D_MODEL))


def _diff_attn_kernel(lam_ref, subln_ref, q_ref, k_ref, v_ref, o_ref,
                      vt_ref, tri_ref, s_ref, bmax_ref, m_ref, acc_ref, *, tq, tk, lam_init):
    i = pl.program_id(2)
    n = 2 * tq

    @pl.when(i == 0)
    def _():
        ones_row = jnp.where(lax.broadcasted_iota(jnp.int32, (SUM_ROWS, tk), 0) == 0, 1.0, 0.0)
        for c in range(vt_ref.shape[0]):
            vt_ref[c, 0:LANES, :] = v_ref[c * tk:(c + 1) * tk, :].astype(F32).T.astype(BF16)
            vt_ref[c, LANES:, :] = ones_row.astype(BF16)
        key = lax.broadcasted_iota(jnp.int32, tri_ref.shape, 0)
        qry = lax.broadcasted_iota(jnp.int32, tri_ref.shape, 1) & (tk - 1)
        tri_ref[...] = jnp.where(key <= qry, 0.0, NEG_BIG)

    def query_columns(step):
        q = q_ref[pl.ds(pl.multiple_of(step * tq, tq), tq), :]
        return _halves_side_by_side(q, tk)

    qcat = query_columns(i)

    m_ref[...] = jnp.full(m_ref.shape, NEG_BIG, F32)
    acc_ref[...] = jnp.zeros(acc_ref.shape, F32)

    def scores(j, lo=0):
        kb = k_ref[pl.ds(pl.multiple_of(j * tk, tk), tk), :]
        return _dot_nt(kb, qcat[lo:])

    def fill(slot, s):
        s_ref[slot] = s
        bmax_ref[slot] = jnp.max(s, axis=0, keepdims=True)

    @pl.when(i == 0)
    def _():
        fill(0, scores(0))

    def update(s, j, lo=0, bmax=None):
        m_old = m_ref[:, lo:]
        if bmax is None:
            bmax = jnp.max(s, axis=0, keepdims=True)
        m_new = jnp.maximum(m_old, bmax)
        alpha = jnp.exp2(m_old - m_new)
        p = jnp.exp2(s - m_new).astype(BF16)
        acc_ref[:, lo:] = alpha * acc_ref[:, lo:] + _dot(vt_ref[j], p)
        m_ref[:, lo:] = m_new

    def pair(j):
        fill(1, scores(j + 1))
        update(s_ref[0], j, bmax=bmax_ref[0])
        fill(0, scores(j + 2))
        update(s_ref[1], j + 1, bmax=bmax_ref[1])

    def body(t, carry):
        pair(4 * t)
        pair(4 * t + 2)
        return carry

    ratio = tq // tk
    below = ratio * i
    lax.fori_loop(0, below >> 2, body, 0)

    if ratio % 4:
        @pl.when((below & 3) == 2)
        def _():
            pair(below - 2)

    tri = tri_ref[...]
    group = 2 * tk
    for d in range(ratio):
        lo = d * group
        if d + 1 < ratio:
            s_ref[(d + 1) % 2, :, lo + group:] = scores(below + d + 1, lo + group)
        s = s_ref[d % 2, :, lo:]
        if d + 1 < ratio:
            s = jnp.concatenate([s[:, :group] + tri, s[:, group:]], axis=1)
        else:
            s = s + tri
        update(s, below + d, lo)

    nxt = jnp.minimum(i + 1, pl.num_programs(2) - 1)
    fill(0, _dot_nt(k_ref[0:tk, :], query_columns(nxt)))

    lp = lam_ref[...]
    lam = (jnp.exp(jnp.sum(lp[0:1] * lp[1:2], axis=-1, keepdims=True))
           - jnp.exp(jnp.sum(lp[2:3] * lp[3:4], axis=-1, keepdims=True)) + lam_init)
    acc = acc_ref[...]
    first, second = _split_side_by_side(acc[:LANES] / acc[LANES:LANES + 1], tk)
    o_t = first - lam * second
    ms = jnp.mean(o_t * o_t, axis=0, keepdims=True)
    o_t = o_t * lax.rsqrt(ms + NORM_EPS) * (subln_ref[...] * (1.0 - lam_init))
    o_ref[...] = o_t.T.astype(BF16)


def _diff_attention(qkv, lam_params, subln_g, *, batch, seq, layer_idx):
    tq, tk = A_TQ, A_TK
    assert tq % (2 * tk) == 0
    nq = seq // tq
    lam_init = 0.8 - 0.6 * math.exp(-0.3 * layer_idx)
    kern = functools.partial(_diff_attn_kernel, tq=tq, tk=tk, lam_init=lam_init)
    return pl.pallas_call(
        kern,
        out_shape=jax.ShapeDtypeStruct((batch * seq, A_HEADS * LANES), BF16),
        grid=(batch, A_HEADS, nq),
        in_specs=[pl.BlockSpec((4, HEAD_DIM), lambda b, h, i: (0, 0)),
                  pl.BlockSpec((LANES, 1), lambda b, h, i: (0, 0)),
                  pl.BlockSpec((seq, LANES), lambda b, h, i: (b, h)),
                  pl.BlockSpec((seq, LANES), lambda b, h, i: (b, A_HEADS + h)),
                  pl.BlockSpec((seq, LANES), lambda b, h, i: (b, 2 * A_HEADS + h))],
        out_specs=pl.BlockSpec((tq, LANES), lambda b, h, i: (b * nq + i, h)),
        scratch_shapes=[pltpu.VMEM((seq // tk, LANES + SUM_ROWS, tk), BF16),
                        pltpu.VMEM((tk, 2 * tk), F32),
                        pltpu.VMEM((2, tk, 2 * tq), F32),
                        pltpu.VMEM((2, 1, 2 * tq), F32),
                        pltpu.VMEM((1, 2 * tq), F32),
                        pltpu.VMEM((LANES + SUM_ROWS, 2 * tq), F32)],
        compiler_params=_params(("arbitrary", "arbitrary", "arbitrary")),
        name="diff_attention",
    )(lam_params, subln_g.reshape(LANES, 1), qkv, qkv, qkv)


def _stick_kernel(q_ref, k_ref, v_ref, o_ref, kp_ref, vt_ref, tri_ref, s_ref, carry_ref, acc_ref,
                  *, tq, tk):
    i = pl.program_id(2)
    n = 2 * tq
    nv = tk // 8

    @pl.when(i == 0)
    def _():
        r = lax.broadcasted_iota(jnp.int32, (tk, tk), 0)
        c = lax.broadcasted_iota(jnp.int32, (tk, tk), 1)
        perm = jnp.where(c == (r & 7) * nv + (r >> 3), 1.0, 0.0).astype(BF16)

        for blk in range(vt_ref.shape[0]):
            rows = slice(blk * tk, (blk + 1) * tk)
            kv = _dot(perm, jnp.concatenate([k_ref[rows, :], v_ref[rows, :]], axis=1))
            kp_ref[rows, :] = kv[:, :LANES].astype(BF16)
            vt_ref[blk] = kv[:, LANES:].T.astype(BF16)
        row = lax.broadcasted_iota(jnp.int32, (tk, tq), 0)
        kpos = (row & 7) * nv + (row >> 3)
        qry = lax.broadcasted_iota(jnp.int32, (tk, tq), 1) & (tk - 1)
        tri_ref[...] = jnp.where(kpos < qry, 0.0, NEG_BIG)

    qcat = _halves_side_by_side(q_ref[...], tk)

    carry_ref[...] = jnp.ones(carry_ref.shape, F32)
    acc_ref[...] = jnp.zeros(acc_ref.shape, F32)

    def scores(j, lo=0):
        kb = kp_ref[pl.ds(pl.multiple_of(j * tk, tk), tk), :]
        return _dot_nt(kb, qcat[lo:])


    def update(zh, j, lo=0):
        w = n - lo
        beta = 0.5 + 0.5 * jnp.tanh(zh)
        run = jnp.ones((8, w), F32)
        bw = [None] * nv
        for v in reversed(range(nv)):
            rows = slice(v * 8, (v + 1) * 8)
            bw[v] = beta[rows] * run
            run = run - bw[v]
        sub = lax.broadcasted_iota(jnp.int32, (8, w), 0)
        later = jnp.ones((8, w), F32)
        for sp in range(7, 0, -1):
            later = jnp.where(sub < sp, later * run[sp:sp + 1, :], later)
        carry = carry_ref[:, lo:]
        scale = later * carry
        a = jnp.concatenate([bw[v] * scale for v in range(nv)], axis=0).astype(BF16)
        acc_ref[:, lo:] += _dot(vt_ref[j], a)
        carry_ref[:, lo:] = carry * (later[0:1, :] * run[0:1, :])

    top = 2 * i + 1

    def block(j):
        return jnp.maximum(j, 0)

    def absent(j):
        return jnp.where(j < 0, NEG_BIG, 0.0)

    s_ref[0, :, tq:] = scores(top, tq)
    s_ref[1] = scores(top - 1)
    tri = tri_ref[...]
    update(s_ref[0, :, tq:] + tri, top, tq)
    s_ref[0] = scores(block(top - 2))
    update(jnp.concatenate([s_ref[1, :, :tq] + tri, s_ref[1, :, tq:]], axis=1), top - 1)
    s_ref[1] = scores(block(top - 3))
    update(s_ref[0] + absent(top - 2), block(top - 2))

    def live():
        return jnp.max(carry_ref[...]) > NEGLIGIBLE

    def cond(state):
        t, alive = state
        return jnp.logical_and(t < i, alive)

    def body(state):
        t, _ = state
        j = top - 3 - 2 * t
        s_ref[0] = scores(block(j - 1))
        update(s_ref[1], j)
        s_ref[1] = scores(block(j - 2))
        update(s_ref[0] + absent(j - 1), block(j - 1))
        return t + 1, live()

    lax.while_loop(cond, body, (0, live()))
    even_head, odd_head = _split_side_by_side(acc_ref[...], tk)
    head_row = lax.broadcasted_iota(jnp.int32, (LANES, tq), 0) < HEAD_DIM
    o_t = jnp.where(head_row, even_head, odd_head)
    o_ref[...] = o_t.T.astype(BF16)


def _stick_attention(qkv, *, batch, seq):
    tq, tk = B_TQ, B_TK
    assert tq == 2 * tk
    nq = seq // tq
    n_tiles = B_HEADS * HEAD_DIM // LANES
    kern = functools.partial(_stick_kernel, tq=tq, tk=tk)
    return pl.pallas_call(
        kern,
        out_shape=jax.ShapeDtypeStruct((batch * seq, B_HEADS * HEAD_DIM), BF16),
        grid=(batch, n_tiles, nq),
        in_specs=[pl.BlockSpec((tq, LANES), lambda b, g, i: (b * nq + i, g)),
                  pl.BlockSpec((seq, LANES), lambda b, g, i: (b, n_tiles + g)),
                  pl.BlockSpec((seq, LANES), lambda b, g, i: (b, 2 * n_tiles + g))],
        out_specs=pl.BlockSpec((tq, LANES), lambda b, g, i: (b * nq + i, g)),
        scratch_shapes=[pltpu.VMEM((seq, LANES), BF16),
                        pltpu.VMEM((seq // tk, LANES, tk), BF16),
                        pltpu.VMEM((tk, tq), F32),
                        pltpu.VMEM((2, tk, 2 * tq), F32),
                        pltpu.VMEM((1, 2 * tq), F32),
                        pltpu.VMEM((LANES, 2 * tq), F32)],
        compiler_params=_params(("arbitrary", "arbitrary", "arbitrary")),
        name="stick_breaking_attention",
    )(qkv, qkv, qkv)


def _swa_kernel(sink_ref, q_ref, kprev_ref, kcur_ref, vprev_ref, vcur_ref, o_ref, *, tq):
    i = pl.program_id(1)
    w = WINDOW
    qi = lax.broadcasted_iota(jnp.int32, (w, 2 * w), 0)
    ki = lax.broadcasted_iota(jnp.int32, (w, 2 * w), 1)
    rel = w + qi - ki
    band = (rel >= 0) & (rel < w)
    lane = lax.broadcasted_iota(jnp.int32, (w, LANES), 1)
    low = lane < HEAD_DIM
    n_qtiles = C_HEADS * HEAD_DIM // LANES
    tiles_per_group = n_qtiles // C_KV_HEADS
    for sub in range(tq // w):
        rows = slice(sub * w, (sub + 1) * w)
        if sub == 0:
            k_prev, v_prev = kprev_ref[...], vprev_ref[...]
            mask = band & ((ki >= w) | (i > 0))
        else:
            prev = slice((sub - 1) * w, sub * w)
            k_prev, v_prev = kcur_ref[prev, :], vcur_ref[prev, :]
            mask = band
        kwin = jnp.concatenate([k_prev, kcur_ref[rows, :]], axis=0)
        vwin = jnp.concatenate([v_prev, vcur_ref[rows, :]], axis=0)
        for t in range(n_qtiles):
            grp = t // tiles_per_group
            kg = kwin[:, grp * LANES:(grp + 1) * LANES]
            vg = vwin[:, grp * LANES:(grp + 1) * LANES]
            q = q_ref[rows, t * LANES:(t + 1) * LANES]
            zero = jnp.zeros_like(q)
            outs = []
            for c in range(2):
                qc = jnp.where(low, q, zero) if c == 0 else jnp.where(low, zero, q)
                sc = jnp.where(mask, _dot_nt(qc, kg), NEG_BIG)
                sink = sink_ref[2 * t + c] * LOG2E
                mx = jnp.maximum(jnp.max(sc, axis=-1, keepdims=True), sink)
                p = jnp.exp2(sc - mx)
                denom = jnp.sum(p, axis=-1, keepdims=True) + jnp.exp2(sink - mx)
                outs.append(_dot(p.astype(BF16), vg) / denom)
            o_ref[rows, t * LANES:(t + 1) * LANES] = jnp.where(low, outs[0], outs[1]).astype(BF16)


def _swa_attention(qkv, sinks, *, batch, seq):
    tq = C_TQ
    w = WINDOW
    nq = seq // tq
    per = tq // w
    nq_cols = C_HEADS * HEAD_DIM
    kcol = nq_cols // (2 * LANES)
    vcol = kcol + 1

    def prev_map(col):
        return lambda b, i: (jnp.maximum((b * nq + i) * per - 1, 0), col)

    def cur_map(col):
        return lambda b, i: (b * nq + i, col)

    return pl.pallas_call(
        functools.partial(_swa_kernel, tq=tq),
        out_shape=jax.ShapeDtypeStruct((batch * seq, nq_cols), BF16),
        grid=(batch, nq),
        in_specs=[pl.BlockSpec(memory_space=pltpu.SMEM),
                  pl.BlockSpec((tq, nq_cols), lambda b, i: (b * nq + i, 0)),
                  pl.BlockSpec((w, 2 * LANES), prev_map(kcol)),
                  pl.BlockSpec((tq, 2 * LANES), cur_map(kcol)),
                  pl.BlockSpec((w, 2 * LANES), prev_map(vcol)),
                  pl.BlockSpec((tq, 2 * LANES), cur_map(vcol))],
        out_specs=pl.BlockSpec((tq, nq_cols), lambda b, i: (b * nq + i, 0)),
        compiler_params=_params(("arbitrary", "arbitrary")),
        name="sliding_window_attention",
    )(sinks, qkv, qkv, qkv, qkv, qkv)


def kernel(x, positions, norm_gains, a_w_in, a_w_out, a_lambda, a_subln, b_w_in, b_w_out,
           c_w_in, c_w_out, c_sinks, ffn_w_gate, ffn_w_up, ffn_w_down):
    batch, seq, d = x.shape
    depth = norm_gains.shape[0]
    tables = _rope_tables(positions)
    h = x.reshape(batch * seq, d)
    a_in, b_in = a_w_in.astype(BF16), b_w_in.astype(BF16)
    w_outs = (a_w_out.astype(BF16), b_w_out.astype(BF16), c_w_out.astype(BF16))
    w_gate, w_up, w_down = (ffn_w_gate.astype(BF16), ffn_w_up.astype(BF16),
                            ffn_w_down.astype(BF16))
    for i in range(depth):
        kind = i % N_MIXERS
        inst = i // N_MIXERS
        g = norm_gains[i]
        if kind == 0:
            qkv = _in_proj(h, g[0], a_in, inst, tables, n_rope=16, n_q=8,
                           q_scale=ATTN_SCALE * LOG2E)
            mix = _diff_attention(qkv, a_lambda[inst], a_subln[inst], batch=batch, seq=seq, layer_idx=i)
        elif kind == 1:
            qkv = _in_proj(h, g[0], b_in, inst, None, n_rope=0, n_q=8,
                           q_scale=ATTN_SCALE * 0.5)
            mix = _stick_attention(qkv, batch=batch, seq=seq)
        else:
            w = c_w_in[inst]
            nq_cols = C_HEADS * HEAD_DIM
            kv = w[:, nq_cols:].reshape(d, 2 * C_KV_HEADS, 1, HEAD_DIM)
            kv = jnp.broadcast_to(kv, (d, 2 * C_KV_HEADS, 2, HEAD_DIM)).reshape(d, 4 * C_KV_HEADS * HEAD_DIM)
            w = jnp.concatenate([w[:, :nq_cols], kv], axis=1).astype(BF16)
            qkv = _in_proj(h, g[0], w[None], 0, tables, n_rope=10, n_q=8,
                           q_scale=ATTN_SCALE * LOG2E)
            mix = _swa_attention(qkv, c_sinks[inst], batch=batch, seq=seq)
        h = _post_mixer(mix, w_outs[kind], inst, g[1], h, g[2], w_gate, w_up, w_down, i, g[3])
    return h.reshape(batch, seq, d)
```

```python
import functools
import math

import jax
import jax.numpy as jnp
from jax import lax
from jax.experimental import pallas as pl
from jax.experimental.pallas import tpu as pltpu

F32 = jnp.float32
BF16 = jnp.bfloat16

D_MODEL = 1024
N_MIXERS = 3
ROPE_THETA = 10000.0
ROPE_DIM = 64
NORM_EPS = 1e-6
HEAD_DIM = 64
ATTN_SCALE = HEAD_DIM ** -0.5
A_HEADS = 8
B_HEADS = 16
C_HEADS = 16
C_KV_HEADS = 2
WINDOW = 128

LANES = 128
NEG_BIG = -1e30
NEGLIGIBLE = 2.0 ** -100
VMEM_LIMIT = 56 * 1024 * 1024

ROW_TILE = 1024
FFN_PARTS = 2
A_TQ = 512
A_TK = 256
B_TQ = 512
B_TK = 256
C_TQ = 512
LOG2E = math.log2(math.e)
SUM_ROWS = 16

def _params(sem):
    return pltpu.CompilerParams(dimension_semantics=sem, vmem_limit_bytes=VMEM_LIMIT)


def _rms(x, g):
    ms = jnp.mean(x * x, axis=-1, keepdims=True)
    return x * lax.rsqrt(ms + NORM_EPS) * g


def _dot(a, b):
    return jnp.dot(a, b, preferred_element_type=F32)


def _dot_nt(a, b):
    return lax.dot_general(a, b, (((1,), (1,)), ((), ())), preferred_element_type=F32)


def _halves_side_by_side(q, blk):
    lane = lax.broadcasted_iota(jnp.int32, q.shape, 1)
    zero = jnp.zeros_like(q)
    first = jnp.where(lane < HEAD_DIM, q, zero)
    second = jnp.where(lane >= HEAD_DIM, q, zero)
    parts = []
    for g in range(q.shape[0] // blk):
        rows = slice(g * blk, (g + 1) * blk)
        parts += [first[rows], second[rows]]
    return jnp.concatenate(parts, axis=0)


def _split_side_by_side(x, blk):
    groups = x.shape[1] // (2 * blk)
    first = [x[:, 2 * g * blk:(2 * g + 1) * blk] for g in range(groups)]
    second = [x[:, (2 * g + 1) * blk:(2 * g + 2) * blk] for g in range(groups)]
    return jnp.concatenate(first, axis=1), jnp.concatenate(second, axis=1)


def _rope_table_kernel(pos_ref, inv_ref, sign_ref, cos_ref, sin_ref):
    ang = pos_ref[...].astype(F32) * inv_ref[...]
    cos_ref[...] = jnp.cos(ang)
    sin_ref[...] = jnp.sin(ang) * sign_ref[...]


def _rope_tables(positions):
    m = positions.size
    inv = ROPE_THETA ** (-jnp.arange(0, ROPE_DIM, 2, dtype=F32) / ROPE_DIM)
    inv128 = jnp.tile(inv, LANES // (ROPE_DIM // 2)).reshape(1, LANES)
    half = ROPE_DIM // 2
    sign = jnp.where((jnp.arange(LANES) % ROPE_DIM) < half, -1.0, 1.0).astype(F32).reshape(1, LANES)
    tm = 2048
    return pl.pallas_call(
        _rope_table_kernel,
        out_shape=(jax.ShapeDtypeStruct((m, LANES), F32), jax.ShapeDtypeStruct((m, LANES), F32)),
        grid=(m // tm,),
        in_specs=[pl.BlockSpec((tm, 1), lambda i: (i, 0)),
                  pl.BlockSpec((1, LANES), lambda i: (0, 0)),
                  pl.BlockSpec((1, LANES), lambda i: (0, 0))],
        out_specs=(pl.BlockSpec((tm, LANES), lambda i: (i, 0)),
                   pl.BlockSpec((tm, LANES), lambda i: (i, 0))),
        compiler_params=_params(("arbitrary",)),
        name="rope_tables",
    )(positions.reshape(m, 1), inv128, sign)


def _swap_halves(x):
    lane = lax.broadcasted_iota(jnp.int32, x.shape, 1)
    fwd = pltpu.roll(x, ROPE_DIM // 2, 1)
    bwd = pltpu.roll(x, LANES - ROPE_DIM // 2, 1)
    return jnp.where((lane & (ROPE_DIM - 1)) < ROPE_DIM // 2, bwd, fwd)


def _in_proj_kernel(*refs, n_tiles, n_rope, n_q, q_scale, chunk):
    if n_rope:
        x_ref, g_ref, w_ref, cos_ref, sin_ref, o_ref = refs
    else:
        x_ref, g_ref, w_ref, o_ref = refs
    half = x_ref.shape[0] // 2
    parts = [slice(r * half, (r + 1) * half) for r in range(2)]
    hns = [_rms(x_ref[rows, :], g_ref[...]).astype(BF16) for rows in parts]
    tiles_per_chunk = chunk // LANES
    for rows, hn in zip(parts, hns):
        if n_rope:
            cos = cos_ref[rows, :]
            sin = sin_ref[rows, :]
        for c in range(n_tiles // tiles_per_chunk):
            acc = _dot(hn, w_ref[:, c * chunk:(c + 1) * chunk])
            for t in range(tiles_per_chunk):
                tile = c * tiles_per_chunk + t
                a = acc[:, t * LANES:(t + 1) * LANES]
                if tile < n_rope:
                    a = a * cos + _swap_halves(a) * sin
                if tile < n_q:
                    a = a * q_scale
                o_ref[rows, tile * LANES:(tile + 1) * LANES] = a.astype(BF16)


def _in_proj(x, g, w_stack, idx, tables, *, n_rope, n_q, q_scale):
    m = x.shape[0]
    n = w_stack.shape[2]
    tm = ROW_TILE
    chunk = 256
    kern = functools.partial(_in_proj_kernel, n_tiles=n // LANES, n_rope=n_rope, n_q=n_q,
                             q_scale=q_scale, chunk=chunk)
    in_specs = [pl.BlockSpec((tm, D_MODEL), lambda i: (i, 0)),
                pl.BlockSpec((1, D_MODEL), lambda i: (0, 0)),
                pl.BlockSpec((None, D_MODEL, n), lambda i: (idx, 0, 0))]
    args = [x, g.reshape(1, D_MODEL), w_stack]
    if n_rope:
        in_specs += [pl.BlockSpec((tm, LANES), lambda i: (i, 0)),
                     pl.BlockSpec((tm, LANES), lambda i: (i, 0))]
        args += list(tables)
    return pl.pallas_call(
        kern,
        out_shape=jax.ShapeDtypeStruct((m, n), BF16),
        grid=(m // tm,),
        in_specs=in_specs,
        out_specs=pl.BlockSpec((tm, n), lambda i: (i, 0)),
        compiler_params=_params(("arbitrary",)),
        name="in_proj",
    )(*args)


def _post_mixer_kernel(mix_ref, wo_ref, gmix_ref, x_ref, gin_ref, wg_ref, wu_ref, wd_ref, gout_ref,
                       o_ref, act_ref, *, chunk):
    hidden = wg_ref.shape[1]
    part = x_ref.shape[0] // FFN_PARTS
    parts = [slice(r * part, (r + 1) * part) for r in range(FFN_PARTS)]
    xs, hns = [], []
    for rows in parts:
        x = x_ref[rows, :] + _rms(_dot(mix_ref[rows, :], wo_ref[...]), gmix_ref[...])
        xs.append(x)
        hns.append(_rms(x, gin_ref[...]).astype(BF16))
    for rows, hn in zip(parts, hns):
        for c in range(hidden // chunk):
            cols = slice(c * chunk, (c + 1) * chunk)
            gate = _dot(hn, wg_ref[:, cols])
            up = _dot(hn, wu_ref[:, cols])
            act_ref[rows, cols] = (gate * jax.nn.sigmoid(gate) * up).astype(BF16)
    for rows, x in zip(parts, xs):
        f = _dot(act_ref[rows, :], wd_ref[...])
        o_ref[rows, :] = x + _rms(f, gout_ref[...])


def _post_mixer(mix, wo, wo_idx, gmix, x, gin, wg, wu, wd, layer, gout):
    m = x.shape[0]
    hidden = wg.shape[2]
    tm = ROW_TILE
    const = lambda i: (0, 0)
    rows = lambda i: (i, 0)

    def resident(shape, which):
        return pl.BlockSpec((None,) + shape, lambda i: (which, 0, 0), pipeline_mode=pl.Buffered(1))

    gain = pl.BlockSpec((1, D_MODEL), const)
    return pl.pallas_call(
        functools.partial(_post_mixer_kernel, chunk=256),
        out_shape=jax.ShapeDtypeStruct((m, D_MODEL), F32),
        grid=(m // tm,),
        in_specs=[pl.BlockSpec((tm, D_MODEL), rows),
                  resident((D_MODEL, D_MODEL), wo_idx),
                  gain,
                  pl.BlockSpec((tm, D_MODEL), rows),
                  gain,
                  resident((D_MODEL, hidden), layer),
                  resident((D_MODEL, hidden), layer),
                  resident((hidden, D_MODEL), layer),
                  gain],
        out_specs=pl.BlockSpec((tm, D_MODEL), rows),
        scratch_shapes=[pltpu.VMEM((tm, hidden), BF16)],
        compiler_params=_params(("arbitrary",)),
        name="out_proj_ffn",
    )(mix, wo, gmix.reshape(1, D_MODEL), x, gin.reshape(1, D_MODEL), wg, wu, wd,
      gout.reshape(1, D_MODEL))


def _diff_attn_kernel(lam_ref, subln_ref, q_ref, k_ref, v_ref, o_ref,
                      vt_ref, tri_ref, s_ref, bmax_ref, m_ref, acc_ref, *, tq, tk, lam_init):
    i = pl.program_id(2)
    n = 2 * tq

    @pl.when(i == 0)
    def _():
        ones_row = jnp.where(lax.broadcasted_iota(jnp.int32, (SUM_ROWS, tk), 0) == 0, 1.0, 0.0)
        for c in range(vt_ref.shape[0]):
            vt_ref[c, 0:LANES, :] = v_ref[c * tk:(c + 1) * tk, :].astype(F32).T.astype(BF16)
            vt_ref[c, LANES:, :] = ones_row.astype(BF16)
        key = lax.broadcasted_iota(jnp.int32, tri_ref.shape, 0)
        qry = lax.broadcasted_iota(jnp.int32, tri_ref.shape, 1) & (tk - 1)
        tri_ref[...] = jnp.where(key <= qry, 0.0, NEG_BIG)

    def query_columns(step):
        q = q_ref[pl.ds(pl.multiple_of(step * tq, tq), tq), :]
        return _halves_side_by_side(q, tk)

    qcat = query_columns(i)

    m_ref[...] = jnp.full(m_ref.shape, NEG_BIG, F32)
    acc_ref[...] = jnp.zeros(acc_ref.shape, F32)

    def scores(j, lo=0):
        kb = k_ref[pl.ds(pl.multiple_of(j * tk, tk), tk), :]
        return _dot_nt(kb, qcat[lo:])

    def fill(slot, s):
        s_ref[slot] = s
        bmax_ref[slot] = jnp.max(s, axis=0, keepdims=True)

    @pl.when(i == 0)
    def _():
        fill(0, scores(0))

    def update(s, j, lo=0, bmax=None):
        m_old = m_ref[:, lo:]
        if bmax is None:
            bmax = jnp.max(s, axis=0, keepdims=True)
        m_new = jnp.maximum(m_old, bmax)
        alpha = jnp.exp2(m_old - m_new)
        p = jnp.exp2(s - m_new).astype(BF16)
        acc_ref[:, lo:] = alpha * acc_ref[:, lo:] + _dot(vt_ref[j], p)
        m_ref[:, lo:] = m_new

    def pair(j):
        fill(1, scores(j + 1))
        update(s_ref[0], j, bmax=bmax_ref[0])
        fill(0, scores(j + 2))
        update(s_ref[1], j + 1, bmax=bmax_ref[1])

    def body(t, carry):
        pair(4 * t)
        pair(4 * t + 2)
        return carry

    ratio = tq // tk
    below = ratio * i
    lax.fori_loop(0, below >> 2, body, 0)

    if ratio % 4:
        @pl.when((below & 3) == 2)
        def _():
            pair(below - 2)

    tri = tri_ref[...]
    group = 2 * tk
    for d in range(ratio):
        lo = d * group
        if d + 1 < ratio:
            s_ref[(d + 1) % 2, :, lo + group:] = scores(below + d + 1, lo + group)
        s = s_ref[d % 2, :, lo:]
        if d + 1 < ratio:
            s = jnp.concatenate([s[:, :group] + tri, s[:, group:]], axis=1)
        else:
            s = s + tri
        update(s, below + d, lo)

    nxt = jnp.minimum(i + 1, pl.num_programs(2) - 1)
    fill(0, _dot_nt(k_ref[0:tk, :], query_columns(nxt)))

    lp = lam_ref[...]
    lam = (jnp.exp(jnp.sum(lp[0:1] * lp[1:2], axis=-1, keepdims=True))
           - jnp.exp(jnp.sum(lp[2:3] * lp[3:4], axis=-1, keepdims=True)) + lam_init)
    acc = acc_ref[...]
    first, second = _split_side_by_side(acc[:LANES] / acc[LANES:LANES + 1], tk)
    o_t = first - lam * second
    ms = jnp.mean(o_t * o_t, axis=0, keepdims=True)
    o_t = o_t * lax.rsqrt(ms + NORM_EPS) * (subln_ref[...] * (1.0 - lam_init))
    o_ref[...] = o_t.T.astype(BF16)


def _diff_attention(qkv, lam_params, subln_g, *, batch, seq, layer_idx):
    tq, tk = A_TQ, A_TK
    assert tq % (2 * tk) == 0
    nq = seq // tq
    lam_init = 0.8 - 0.6 * math.exp(-0.3 * layer_idx)
    kern = functools.partial(_diff_attn_kernel, tq=tq, tk=tk, lam_init=lam_init)
    return pl.pallas_call(
        kern,
        out_shape=jax.ShapeDtypeStruct((batch * seq, A_HEADS * LANES), BF16),
        grid=(batch, A_HEADS, nq),
        in_specs=[pl.BlockSpec((4, HEAD_DIM), lambda b, h, i: (0, 0)),
                  pl.BlockSpec((LANES, 1), lambda b, h, i: (0, 0)),
                  pl.BlockSpec((seq, LANES), lambda b, h, i: (b, h)),
                  pl.BlockSpec((seq, LANES), lambda b, h, i: (b, A_HEADS + h)),
                  pl.BlockSpec((seq, LANES), lambda b, h, i: (b, 2 * A_HEADS + h))],
        out_specs=pl.BlockSpec((tq, LANES), lambda b, h, i: (b * nq + i, h)),
        scratch_shapes=[pltpu.VMEM((seq // tk, LANES + SUM_ROWS, tk), BF16),
                        pltpu.VMEM((tk, 2 * tk), F32),
                        pltpu.VMEM((2, tk, 2 * tq), F32),
                        pltpu.VMEM((2, 1, 2 * tq), F32),
                        pltpu.VMEM((1, 2 * tq), F32),
                        pltpu.VMEM((LANES + SUM_ROWS, 2 * tq), F32)],
        compiler_params=_params(("arbitrary", "arbitrary", "arbitrary")),
        name="diff_attention",
    )(lam_params, subln_g.reshape(LANES, 1), qkv, qkv, qkv)


def _stick_kernel(q_ref, k_ref, v_ref, o_ref, kp_ref, vt_ref, tri_ref, s_ref, carry_ref, acc_ref,
                  *, tq, tk):
    i = pl.program_id(2)
    n = 2 * tq
    nv = tk // 8

    @pl.when(i == 0)
    def _():
        r = lax.broadcasted_iota(jnp.int32, (tk, tk), 0)
        c = lax.broadcasted_iota(jnp.int32, (tk, tk), 1)
        perm = jnp.where(c == (r & 7) * nv + (r >> 3), 1.0, 0.0).astype(BF16)

        for blk in range(vt_ref.shape[0]):
            rows = slice(blk * tk, (blk + 1) * tk)
            kv = _dot(perm, jnp.concatenate([k_ref[rows, :], v_ref[rows, :]], axis=1))
            kp_ref[rows, :] = kv[:, :LANES].astype(BF16)
            vt_ref[blk] = kv[:, LANES:].T.astype(BF16)
        row = lax.broadcasted_iota(jnp.int32, (tk, tq), 0)
        kpos = (row & 7) * nv + (row >> 3)
        qry = lax.broadcasted_iota(jnp.int32, (tk, tq), 1) & (tk - 1)
        tri_ref[...] = jnp.where(kpos < qry, 0.0, NEG_BIG)

    qcat = _halves_side_by_side(q_ref[...], tk)

    carry_ref[...] = jnp.ones(carry_ref.shape, F32)
    acc_ref[...] = jnp.zeros(acc_ref.shape, F32)

    def scores(j, lo=0):
        kb = kp_ref[pl.ds(pl.multiple_of(j * tk, tk), tk), :]
        return _dot_nt(kb, qcat[lo:])


    def update(zh, j, lo=0):
        w = n - lo
        beta = 0.5 + 0.5 * jnp.tanh(zh)
        run = jnp.ones((8, w), F32)
        bw = [None] * nv
        for v in reversed(range(nv)):
            rows = slice(v * 8, (v + 1) * 8)
            bw[v] = beta[rows] * run
            run = run - bw[v]
        sub = lax.broadcasted_iota(jnp.int32, (8, w), 0)
        later = jnp.ones((8, w), F32)
        for sp in range(7, 0, -1):
            later = jnp.where(sub < sp, later * run[sp:sp + 1, :], later)
        carry = carry_ref[:, lo:]
        scale = later * carry
        a = jnp.concatenate([bw[v] * scale for v in range(nv)], axis=0).astype(BF16)
        acc_ref[:, lo:] += _dot(vt_ref[j], a)
        carry_ref[:, lo:] = carry * (later[0:1, :] * run[0:1, :])

    top = 2 * i + 1

    def block(j):
        return jnp.maximum(j, 0)

    def absent(j):
        return jnp.where(j < 0, NEG_BIG, 0.0)

    s_ref[0, :, tq:] = scores(top, tq)
    s_ref[1] = scores(top - 1)
    tri = tri_ref[...]
    update(s_ref[0, :, tq:] + tri, top, tq)
    s_ref[0] = scores(block(top - 2))
    update(jnp.concatenate([s_ref[1, :, :tq] + tri, s_ref[1, :, tq:]], axis=1), top - 1)
    s_ref[1] = scores(block(top - 3))
    update(s_ref[0] + absent(top - 2), block(top - 2))

    def live():
        return jnp.max(carry_ref[...]) > NEGLIGIBLE

    def cond(state):
        t, alive = state
        return jnp.logical_and(t < i, alive)

    def body(state):
        t, _ = state
        j = top - 3 - 2 * t
        s_ref[0] = scores(block(j - 1))
        update(s_ref[1], j)
        s_ref[1] = scores(block(j - 2))
        update(s_ref[0] + absent(j - 1), block(j - 1))
        return t + 1, live()

    lax.while_loop(cond, body, (0, live()))
    even_head, odd_head = _split_side_by_side(acc_ref[...], tk)
    head_row = lax.broadcasted_iota(jnp.int32, (LANES, tq), 0) < HEAD_DIM
    o_t = jnp.where(head_row, even_head, odd_head)
    o_ref[...] = o_t.T.astype(BF16)


def _stick_attention(qkv, *, batch, seq):
    tq, tk = B_TQ, B_TK
    assert tq == 2 * tk
    nq = seq // tq
    n_tiles = B_HEADS * HEAD_DIM // LANES
    kern = functools.partial(_stick_kernel, tq=tq, tk=tk)
    return pl.pallas_call(
        kern,
        out_shape=jax.ShapeDtypeStruct((batch * seq, B_HEADS * HEAD_DIM), BF16),
        grid=(batch, n_tiles, nq),
        in_specs=[pl.BlockSpec((tq, LANES), lambda b, g, i: (b * nq + i, g)),
                  pl.BlockSpec((seq, LANES), lambda b, g, i: (b, n_tiles + g)),
                  pl.BlockSpec((seq, LANES), lambda b, g, i: (b, 2 * n_tiles + g))],
        out_specs=pl.BlockSpec((tq, LANES), lambda b, g, i: (b * nq + i, g)),
        scratch_shapes=[pltpu.VMEM((seq, LANES), BF16),
                        pltpu.VMEM((seq // tk, LANES, tk), BF16),
                        pltpu.VMEM((tk, tq), F32),
                        pltpu.VMEM((2, tk, 2 * tq), F32),
                        pltpu.VMEM((1, 2 * tq), F32),
                        pltpu.VMEM((LANES, 2 * tq), F32)],
        compiler_params=_params(("arbitrary", "arbitrary", "arbitrary")),
        name="stick_breaking_attention",
    )(qkv, qkv, qkv)


def _swa_kernel(sink_ref, q_ref, kprev_ref, kcur_ref, vprev_ref, vcur_ref, o_ref, *, tq):
    i = pl.program_id(1)
    w = WINDOW
    qi = lax.broadcasted_iota(jnp.int32, (w, 2 * w), 0)
    ki = lax.broadcasted_iota(jnp.int32, (w, 2 * w), 1)
    rel = w + qi - ki
    band = (rel >= 0) & (rel < w)
    lane = lax.broadcasted_iota(jnp.int32, (w, LANES), 1)
    low = lane < HEAD_DIM
    n_qtiles = C_HEADS * HEAD_DIM // LANES
    tiles_per_group = n_qtiles // C_KV_HEADS
    for sub in range(tq // w):
        rows = slice(sub * w, (sub + 1) * w)
        if sub == 0:
            k_prev, v_prev = kprev_ref[...], vprev_ref[...]
            mask = band & ((ki >= w) | (i > 0))
        else:
            prev = slice((sub - 1) * w, sub * w)
            k_prev, v_prev = kcur_ref[prev, :], vcur_ref[prev, :]
            mask = band
        kwin = jnp.concatenate([k_prev, kcur_ref[rows, :]], axis=0)
        vwin = jnp.concatenate([v_prev, vcur_ref[rows, :]], axis=0)
        for t in range(n_qtiles):
            grp = t // tiles_per_group
            kg = kwin[:, grp * LANES:(grp + 1) * LANES]
            vg = vwin[:, grp * LANES:(grp + 1) * LANES]
            q = q_ref[rows, t * LANES:(t + 1) * LANES]
            zero = jnp.zeros_like(q)
            outs = []
            for c in range(2):
                qc = jnp.where(low, q, zero) if c == 0 else jnp.where(low, zero, q)
                sc = jnp.where(mask, _dot_nt(qc, kg), NEG_BIG)
                sink = sink_ref[2 * t + c] * LOG2E
                mx = jnp.maximum(jnp.max(sc, axis=-1, keepdims=True), sink)
                p = jnp.exp2(sc - mx)
                denom = jnp.sum(p, axis=-1, keepdims=True) + jnp.exp2(sink - mx)
                outs.append(_dot(p.astype(BF16), vg) / denom)
            o_ref[rows, t * LANES:(t + 1) * LANES] = jnp.where(low, outs[0], outs[1]).astype(BF16)


def _swa_attention(qkv, sinks, *, batch, seq):
    tq = C_TQ
    w = WINDOW
    nq = seq // tq
    per = tq // w
    nq_cols = C_HEADS * HEAD_DIM
    kcol = nq_cols // (2 * LANES)
    vcol = kcol + 1

    def prev_map(col):
        return lambda b, i: (jnp.maximum((b * nq + i) * per - 1, 0), col)

    def cur_map(col):
        return lambda b, i: (b * nq + i, col)

    return pl.pallas_call(
        functools.partial(_swa_kernel, tq=tq),
        out_shape=jax.ShapeDtypeStruct((batch * seq, nq_cols), BF16),
        grid=(batch, nq),
        in_specs=[pl.BlockSpec(memory_space=pltpu.SMEM),
                  pl.BlockSpec((tq, nq_cols), lambda b, i: (b * nq + i, 0)),
                  pl.BlockSpec((w, 2 * LANES), prev_map(kcol)),
                  pl.BlockSpec((tq, 2 * LANES), cur_map(kcol)),
                  pl.BlockSpec((w, 2 * LANES), prev_map(vcol)),
                  pl.BlockSpec((tq, 2 * LANES), cur_map(vcol))],
        out_specs=pl.BlockSpec((tq, nq_cols), lambda b, i: (b * nq + i, 0)),
        compiler_params=_params(("arbitrary", "arbitrary")),
        name="sliding_window_attention",
    )(sinks, qkv, qkv, qkv, qkv, qkv)


def kernel(x, positions, norm_gains, a_w_in, a_w_out, a_lambda, a_subln, b_w_in, b_w_out,
           c_w_in, c_w_out, c_sinks, ffn_w_gate, ffn_w_up, ffn_w_down):
    batch, seq, d = x.shape
    depth = norm_gains.shape[0]
    tables = _rope_tables(positions)
    h = x.reshape(batch * seq, d)
    a_in, b_in = a_w_in.astype(BF16), b_w_in.astype(BF16)
    w_outs = (a_w_out.astype(BF16), b_w_out.astype(BF16), c_w_out.astype(BF16))
    w_gate, w_up, w_down = (ffn_w_gate.astype(BF16), ffn_w_up.astype(BF16),
                            ffn_w_down.astype(BF16))
    for i in range(depth):
        kind = i % N_MIXERS
        inst = i // N_MIXERS
        g = norm_gains[i]
        if kind == 0:
            qkv = _in_proj(h, g[0], a_in, inst, tables, n_rope=16, n_q=8,
                           q_scale=ATTN_SCALE * LOG2E)
            mix = _diff_attention(qkv, a_lambda[inst], a_subln[inst], batch=batch, seq=seq, layer_idx=i)
        elif kind == 1:
            qkv = _in_proj(h, g[0], b_in, inst, None, n_rope=0, n_q=8,
                           q_scale=ATTN_SCALE * 0.5)
            mix = _stick_attention(qkv, batch=batch, seq=seq)
        else:
            w = c_w_in[inst]
            nq_cols = C_HEADS * HEAD_DIM
            kv = w[:, nq_cols:].reshape(d, 2 * C_KV_HEADS, 1, HEAD_DIM)
            kv = jnp.broadcast_to(kv, (d, 2 * C_KV_HEADS, 2, HEAD_DIM)).reshape(d, 4 * C_KV_HEADS * HEAD_DIM)
            w = jnp.concatenate([w[:, :nq_cols], kv], axis=1).astype(BF16)
            qkv = _in_proj(h, g[0], w[None], 0, tables, n_rope=10, n_q=8,
                           q_scale=ATTN_SCALE * LOG2E)
            mix = _swa_attention(qkv, c_sinks[inst], batch=batch, seq=seq)
        h = _post_mixer(mix, w_outs[kind], inst, g[1], h, g[2], w_gate, w_up, w_down, i, g[3])
    return h.reshape(batch, seq, d)
```
